```python
import jax, jax.numpy as jnp
from jax import lax
import numpy as np

D_MODEL = 2048
BATCH = 16
SEQ = 2048
DEPTH = 1

HEAD_DIM = 128
ATTN_PATTERNS = ((128, 1), (512, 4), (2048, 16))
N_GROUPS = len(ATTN_PATTERNS)
HEADS_PER_GROUP = 4
ATTN_QKV = N_GROUPS * HEADS_PER_GROUP * HEAD_DIM
ATTN_OUT = HEADS_PER_GROUP * HEAD_DIM
BLK = 128
CONV_WIDTH = 1024
CONV_K = 3
MEM_LEN = 256
MEM_HEADS = 4
MEM_HEAD_DIM = 256
MEM_W = MEM_HEADS * MEM_HEAD_DIM
N_BRANCH = 3
EPS = 1e-6

SPLIT_SIZES = (ATTN_QKV, ATTN_QKV, ATTN_QKV, ATTN_OUT,
               CONV_WIDTH, CONV_WIDTH, CONV_WIDTH, CONV_WIDTH,
               MEM_W, MEM_W, N_BRANCH * D_MODEL)
IN_COLS = int(sum(SPLIT_SIZES))
SPLIT_IDX = [int(i) for i in np.cumsum(SPLIT_SIZES)[:-1]]

kernel_name = "hybrid_dilated_attn_shortconv_memory_gated_merge"


def rms_norm(t, g):
    tf = t.astype(jnp.float32)
    y = tf * lax.rsqrt(jnp.mean(tf * tf, axis=-1, keepdims=True) + EPS) * g.astype(jnp.float32)
    return y.astype(t.dtype)


def banded_causal_attn(q, k, v, span):
    N, L, H, E = q.shape
    nb = -(-L // BLK)
    pad = nb * BLK - L
    q = jnp.pad(q, ((0, 0), (0, pad), (0, 0), (0, 0)))
    k = jnp.pad(k, ((0, 0), (BLK, pad), (0, 0), (0, 0)))
    v = jnp.pad(v, ((0, 0), (BLK, pad), (0, 0), (0, 0)))
    qb = q.reshape(N, nb, BLK, H, E)

    def two_blocks(t):
        t = t.reshape(N, nb + 1, BLK, H, E)
        return jnp.concatenate([t[:, :-1], t[:, 1:]], axis=2)

    kb, vb = two_blocks(k), two_blocks(v)
    s = jnp.einsum('nbqhe,nbkhe->nbhqk', qb.astype(jnp.float32), kb.astype(jnp.float32)) * (E ** -0.5)
    qpos = jnp.arange(BLK)[:, None] + BLK
    kpos = jnp.arange(2 * BLK)[None, :]
    rel = qpos - kpos
    band = (rel >= 0) & (rel <= span)
    valid = (jnp.arange(nb)[:, None] * BLK + kpos - BLK) >= 0
    mask = band[None] & valid[:, None, :]
    s = jnp.where(mask[None, :, None], s, -jnp.inf)
    m = jnp.max(s, axis=-1, keepdims=True)
    p = jnp.exp(s - m)
    den = jnp.sum(p, axis=-1, keepdims=True)
    o = jnp.einsum('nbhqk,nbkhe->nbqhe', p / den, vb.astype(jnp.float32))
    lse = (m + jnp.log(den))[..., 0]
    o = o.reshape(N, nb * BLK, H, E)[:, :L]
    lse = lse.transpose(0, 1, 3, 2).reshape(N, nb * BLK, H)[:, :L]
    return o, lse


def dilated_causal_attn(q, k, v, window, dilation):
    B, S, H, E = q.shape
    L = S // dilation

    def to_classes(t):
        return t.reshape(B, L, dilation, H, E).transpose(0, 2, 1, 3, 4).reshape(B * dilation, L, H, E)

    o, lse = banded_causal_attn(to_classes(q), to_classes(k), to_classes(v), window // dilation)
    o = o.reshape(B, dilation, L, H, E).transpose(0, 2, 1, 3, 4).reshape(B, S, H, E)
    lse = lse.reshape(B, dilation, L, H).transpose(0, 2, 1, 3).reshape(B, S, H)
    return o, lse


def setup_inputs(seed: int = 0) -> dict:
    key = jax.random.key(seed)
    ks = jax.random.split(key, 16)
    f32 = jnp.float32

    def w(k, shape, fan_in):
        return jax.random.normal(k, shape, f32) * (fan_in ** -0.5)

    def gain(k, shape):
        return 1.0 + 0.02 * jax.random.normal(k, shape, f32)

    return {
        "x": jax.random.normal(ks[0], (BATCH, SEQ, D_MODEL), f32),
        "mem": jax.random.normal(ks[1], (BATCH, MEM_LEN, D_MODEL), f32),
        "norm_g": gain(ks[2], (D_MODEL,)),
        "mem_norm_g": gain(ks[3], (D_MODEL,)),
        "w_in": w(ks[4], (D_MODEL, IN_COLS), D_MODEL),
        "attn_q_norm": gain(ks[5], (N_GROUPS, HEAD_DIM)),
        "attn_k_norm": gain(ks[6], (N_GROUPS, HEAD_DIM)),
        "conv_w": w(ks[7], (CONV_K, CONV_WIDTH), CONV_K),
        "mem_w_kv": w(ks[8], (D_MODEL, 2 * MEM_W), D_MODEL),
        "mem_q_norm": gain(ks[9], (MEM_HEAD_DIM,)),
        "mem_k_norm": gain(ks[10], (MEM_HEAD_DIM,)),
        "w_br_attn": w(ks[11], (ATTN_OUT, D_MODEL), ATTN_OUT),
        "w_br_conv": w(ks[12], (CONV_WIDTH, D_MODEL), CONV_WIDTH),
        "w_br_mem": w(ks[13], (MEM_W, D_MODEL), MEM_W),
        "w_out": w(ks[14], (D_MODEL, D_MODEL), D_MODEL),
    }


def reference(x, mem, norm_g, mem_norm_g, w_in, attn_q_norm, attn_k_norm, conv_w,
              mem_w_kv, mem_q_norm, mem_k_norm, w_br_attn, w_br_conv, w_br_mem, w_out):
    B, S, D = x.shape
    for _layer in range(DEPTH):
        h = rms_norm(x, norm_g)
        proj = jnp.einsum('bsd,dc->bsc', h, w_in)
        (q, k, v, z_attn, conv_b, conv_c, conv_v, z_conv,
         mem_q, z_mem, gates) = jnp.split(proj, SPLIT_IDX, axis=-1)

        q = q.reshape(B, S, N_GROUPS, HEADS_PER_GROUP, HEAD_DIM)
        k = k.reshape(B, S, N_GROUPS, HEADS_PER_GROUP, HEAD_DIM)
        v = v.reshape(B, S, N_GROUPS, HEADS_PER_GROUP, HEAD_DIM)
        outs, lses = [], []
        for g, (window, dilation) in enumerate(ATTN_PATTERNS):
            qg = rms_norm(q[:, :, g], attn_q_norm[g])
            kg = rms_norm(k[:, :, g], attn_k_norm[g])
            o, lse = dilated_causal_attn(qg, kg, v[:, :, g], window, dilation)
            outs.append(o)
            lses.append(lse)
        alpha = jax.nn.softmax(jnp.stack(lses, axis=0), axis=0)
        a = jnp.sum(alpha[..., None] * jnp.stack(outs, axis=0), axis=0)
        a = a.reshape(B, S, ATTN_OUT).astype(x.dtype) * jax.nn.silu(z_attn)

        u = conv_c * conv_v
        y = sum(conv_w[j] * jnp.pad(u, ((0, 0), (j, 0), (0, 0)))[:, :S] for j in range(CONV_K))
        c = conv_b * y * jax.nn.silu(z_conv)

        mh = rms_norm(mem, mem_norm_g)
        mkv = jnp.einsum('bmd,dc->bmc', mh, mem_w_kv)
        mk, mv = jnp.split(mkv, 2, axis=-1)
        M = mem.shape[1]
        mq = rms_norm(mem_q.reshape(B, S, MEM_HEADS, MEM_HEAD_DIM), mem_q_norm)
        mk = rms_norm(mk.reshape(B, M, MEM_HEADS, MEM_HEAD_DIM), mem_k_norm)
        mv = mv.reshape(B, M, MEM_HEADS, MEM_HEAD_DIM)
        ms = jnp.einsum('bshe,bmhe->bhsm', mq.astype(jnp.float32), mk.astype(jnp.float32)) * (MEM_HEAD_DIM ** -0.5)
        mp = jax.nn.softmax(ms, axis=-1)
        mo = jnp.einsum('bhsm,bmhe->bshe', mp, mv.astype(jnp.float32))
        mo = mo.reshape(B, S, MEM_W).astype(x.dtype) * jax.nn.silu(z_mem)

        gt = jax.nn.sigmoid(gates.astype(jnp.float32).reshape(B, S, N_BRANCH, D)).astype(x.dtype)
        merged = (gt[:, :, 0] * jnp.einsum('bsc,cd->bsd', a, w_br_attn)
                  + gt[:, :, 1] * jnp.einsum('bsc,cd->bsd', c, w_br_conv)
                  + gt[:, :, 2] * jnp.einsum('bsc,cd->bsd', mo, w_br_mem))
        x = x + jnp.einsum('bsd,de->bse', merged, w_out)
    return x
```

```python
import functools

import jax
import jax.numpy as jnp
from jax import lax
from jax.experimental import pallas as pl
from jax.experimental.pallas import tpu as pltpu

D_MODEL = 2048
SEQ = 2048
HEAD_DIM = 128
ATTN_PATTERNS = ((128, 1), (512, 4), (2048, 16))
N_GROUPS = len(ATTN_PATTERNS)
HEADS_PER_GROUP = 4
GROUP_W = HEADS_PER_GROUP * HEAD_DIM
ATTN_QKV = N_GROUPS * GROUP_W
ATTN_OUT = GROUP_W
BLK = 128
CONV_WIDTH = 1024
CONV_K = 3
MEM_LEN = 256
MEM_HEADS = 4
MEM_HEAD_DIM = 256
MEM_W = MEM_HEADS * MEM_HEAD_DIM
N_BRANCH = 3
EPS = 1e-6
IN_COLS = 3 * ATTN_QKV + ATTN_OUT + 4 * CONV_WIDTH + 2 * MEM_W + N_BRANCH * D_MODEL

COL_Q = 0
COL_K = ATTN_QKV
COL_V = 2 * ATTN_QKV
COL_ZA = 3 * ATTN_QKV
COL_CB = COL_ZA + ATTN_OUT
COL_CC = COL_CB + CONV_WIDTH
COL_CV = COL_CC + CONV_WIDTH
COL_ZC = COL_CV + CONV_WIDTH
COL_MQ = COL_ZC + CONV_WIDTH
COL_ZM = COL_MQ + MEM_W
COL_G = COL_ZM + MEM_W

F32 = jnp.float32
BF16 = jnp.bfloat16

VMEM_LIMIT_BYTES = 56 * 1024 * 1024

PROJ_TM = 1024
PROJ_TN = 1024
MERGE_TS = 256
HALO = 16
ROW_CHUNK = 256


def _rms(t, gain):
    return t * lax.rsqrt(jnp.mean(t * t, axis=-1, keepdims=True) + EPS) * gain


def _silu(z):
    return z * jax.nn.sigmoid(z)


def _in_proj_kernel(x_ref, g_ref, w_ref, o_ref, h_ref):
    @pl.when(pl.program_id(1) == 0)
    def _():
        def body(c, _):
            rows = pl.ds(pl.multiple_of(c * ROW_CHUNK, ROW_CHUNK), ROW_CHUNK)
            h_ref[rows, :] = _rms(x_ref[rows, :], g_ref[...]).astype(BF16)
            return _
        lax.fori_loop(0, PROJ_TM // ROW_CHUNK, body, None)

    o_ref[...] = jnp.dot(h_ref[...], w_ref[...], preferred_element_type=F32).astype(o_ref.dtype)


def _in_proj(x2, g, w):
    n = x2.shape[0]
    return pl.pallas_call(
        _in_proj_kernel,
        grid=(n // PROJ_TM, IN_COLS // PROJ_TN),
        in_specs=[
            pl.BlockSpec((PROJ_TM, D_MODEL), lambda i, j: (i, 0)),
            pl.BlockSpec((1, D_MODEL), lambda i, j: (0, 0)),
            pl.BlockSpec((D_MODEL, PROJ_TN), lambda i, j: (0, j)),
        ],
        out_specs=pl.BlockSpec((PROJ_TM, PROJ_TN), lambda i, j: (i, j)),
        out_shape=jax.ShapeDtypeStruct((n, IN_COLS), BF16),
        scratch_shapes=[pltpu.VMEM((PROJ_TM, D_MODEL), BF16)],
        compiler_params=pltpu.CompilerParams(
            dimension_semantics=("arbitrary", "arbitrary"),
            vmem_limit_bytes=VMEM_LIMIT_BYTES),
        name="in_proj",
    )(x2, g, w)


def _mem_kv_kernel(mem_ref, g_ref, w_ref, kn_ref, mk_ref, mv_ref):
    mh = _rms(mem_ref[...], g_ref[...]).astype(BF16)
    kv = jnp.dot(mh, w_ref[...], preferred_element_type=F32)
    for h in range(MEM_HEADS):
        cols = slice(h * MEM_HEAD_DIM, (h + 1) * MEM_HEAD_DIM)
        mk_ref[:, cols] = _rms(kv[:, cols], kn_ref[...]).astype(BF16)
    mv_ref[...] = kv[:, MEM_W:].astype(BF16)


def _mem_kv(mem, g, w, kn):
    b = mem.shape[0]
    return pl.pallas_call(
        _mem_kv_kernel,
        grid=(b,),
        in_specs=[
            pl.BlockSpec((None, MEM_LEN, D_MODEL), lambda i: (i, 0, 0)),
            pl.BlockSpec((1, D_MODEL), lambda i: (0, 0)),
            pl.BlockSpec((D_MODEL, 2 * MEM_W), lambda i: (0, 0)),
            pl.BlockSpec((1, MEM_HEAD_DIM), lambda i: (0, 0)),
        ],
        out_specs=[
            pl.BlockSpec((None, MEM_LEN, MEM_W), lambda i: (i, 0, 0)),
            pl.BlockSpec((None, MEM_LEN, MEM_W), lambda i: (i, 0, 0)),
        ],
        out_shape=[jax.ShapeDtypeStruct((b, MEM_LEN, MEM_W), BF16)] * 2,
        compiler_params=pltpu.CompilerParams(
            dimension_semantics=("arbitrary",),
            vmem_limit_bytes=VMEM_LIMIT_BYTES),
        name="mem_kv",
    )(mem, g, w, kn)


def _rows(start, size, stride):
    return pl.ds(start, size) if stride == 1 else pl.ds(start, size, stride=stride)


def _attn_block(h, r, bi, d, has_prev, first, sq_ref, sk_ref, sv_ref, acc_ref, m_ref, l_ref):
    q0 = bi * (BLK * d) + r
    rows_q = _rows(q0, BLK, d)
    nk = 2 * BLK if has_prev else BLK
    k0 = q0 - BLK * d if has_prev else q0
    rows_k = _rows(k0, nk, d)
    qb = sq_ref[rows_q, :].astype(BF16)
    kb = sk_ref[rows_k, :].astype(BF16)
    vb = sv_ref[rows_k, :].astype(BF16)
    s = lax.dot_general(qb, kb, (((1,), (1,)), ((), ())), preferred_element_type=F32)
    qi = lax.broadcasted_iota(jnp.int32, (BLK, nk), 0)
    kj = lax.broadcasted_iota(jnp.int32, (BLK, nk), 1)
    if has_prev:
        valid = lax.bitcast_convert_type(kj - qi, jnp.uint32) <= jnp.uint32(BLK)
    else:
        valid = kj <= qi
    s = jnp.where(valid, s, -jnp.inf)
    mb = jnp.max(s, axis=-1, keepdims=True)
    if first:
        m_new = jnp.broadcast_to(mb, (BLK, HEAD_DIM))
    else:
        m_old = m_ref[h, rows_q, :]
        m_new = jnp.maximum(m_old, mb)
    m_wide = jnp.concatenate([m_new, m_new], axis=-1) if has_prev else m_new
    p = jnp.exp(s - m_wide)
    psum = jnp.sum(p, axis=-1, keepdims=True)
    pv = jnp.dot(p.astype(BF16), vb, preferred_element_type=F32)
    if first:
        l_new = jnp.broadcast_to(psum, (BLK, HEAD_DIM))
        acc_new = pv
    else:
        alpha = jnp.exp(m_old - m_new)
        l_new = alpha * l_ref[h, rows_q, :] + psum
        acc_new = alpha * acc_ref[h, rows_q, :] + pv
    m_ref[h, rows_q, :] = m_new
    l_ref[h, rows_q, :] = l_new
    acc_ref[h, rows_q, :] = acc_new


def _attn_group(grp, d, first, q_ref, k_ref, v_ref, qn_ref, kn_ref,
                acc_ref, m_ref, l_ref, sq_ref, sk_ref, sv_ref):
    nblk = SEQ // d // BLK
    gq = qn_ref[grp:grp + 1, :] * (HEAD_DIM ** -0.5)
    gk = kn_ref[grp:grp + 1, :]
    for h in range(HEADS_PER_GROUP):
        cols = slice(h * HEAD_DIM, (h + 1) * HEAD_DIM)

        def stage(c, _):
            rows = pl.ds(pl.multiple_of(c * ROW_CHUNK, ROW_CHUNK), ROW_CHUNK)
            sq_ref[rows, :] = _rms(q_ref[rows, cols].astype(F32), gq)
            sk_ref[rows, :] = _rms(k_ref[rows, cols].astype(F32), gk)
            sv_ref[rows, :] = v_ref[rows, cols].astype(F32)
            return _
        lax.fori_loop(0, SEQ // ROW_CHUNK, stage, None)

        block = functools.partial(
            _attn_block, d=d, first=first, sq_ref=sq_ref, sk_ref=sk_ref, sv_ref=sv_ref,
            acc_ref=acc_ref, m_ref=m_ref, l_ref=l_ref)

        def class_body(r, _):
            block(h, r, 0, has_prev=False)
            if nblk > 1:
                def blk_body(bi, _):
                    block(h, r, bi, has_prev=True)
                    return _
                lax.fori_loop(1, nblk, blk_body, None)
            return _

        if d == 1:
            class_body(0, None)
        else:
            lax.fori_loop(0, d, class_body, None)


def _attn_kernel(q_ref, k_ref, v_ref, z_ref, qn_ref, kn_ref, o_ref,
                 acc_ref, m_ref, l_ref, sq_ref, sk_ref, sv_ref):
    step = pl.program_id(1)
    for s in range(N_GROUPS):
        grp = N_GROUPS - 1 - s

        @pl.when(step == s)
        def _(s=s, grp=grp):
            _attn_group(grp, ATTN_PATTERNS[grp][1], s == 0, q_ref, k_ref, v_ref, qn_ref, kn_ref,
                        acc_ref, m_ref, l_ref, sq_ref, sk_ref, sv_ref)

    @pl.when(step == N_GROUPS - 1)
    def _():
        for h in range(HEADS_PER_GROUP):
            cols = slice(h * HEAD_DIM, (h + 1) * HEAD_DIM)

            def fin(c, _):
                rows = pl.ds(pl.multiple_of(c * ROW_CHUNK, ROW_CHUNK), ROW_CHUNK)
                a = acc_ref[h, rows, :] / l_ref[h, rows, :]
                o_ref[rows, cols] = (a * _silu(z_ref[rows, cols].astype(F32))).astype(o_ref.dtype)
                return _
            lax.fori_loop(0, SEQ // ROW_CHUNK, fin, None)


def _attn(proj, qn, kn):
    n = proj.shape[0]
    b = n // SEQ
    qb, kb, vb, zb = (c // GROUP_W for c in (COL_Q, COL_K, COL_V, COL_ZA))

    def grp_spec(base):
        return pl.BlockSpec((SEQ, GROUP_W), lambda i, s: (i, base + N_GROUPS - 1 - s))

    slab = pltpu.VMEM((HEADS_PER_GROUP, SEQ, HEAD_DIM), F32)
    stage = pltpu.VMEM((SEQ, HEAD_DIM), F32)
    return pl.pallas_call(
        _attn_kernel,
        grid=(b, N_GROUPS),
        in_specs=[
            grp_spec(qb), grp_spec(kb), grp_spec(vb),
            pl.BlockSpec((SEQ, GROUP_W), lambda i, s: (i, zb)),
            pl.BlockSpec((N_GROUPS, HEAD_DIM), lambda i, s: (0, 0)),
            pl.BlockSpec((N_GROUPS, HEAD_DIM), lambda i, s: (0, 0)),
        ],
        out_specs=pl.BlockSpec((SEQ, ATTN_OUT), lambda i, s: (i, 0)),
        out_shape=jax.ShapeDtypeStruct((n, ATTN_OUT), BF16),
        scratch_shapes=[slab, slab, slab, stage, stage, stage],
        compiler_params=pltpu.CompilerParams(
            dimension_semantics=("arbitrary", "arbitrary"),
            vmem_limit_bytes=VMEM_LIMIT_BYTES),
        name="attn",
    )(proj, proj, proj, proj, qn, kn)


def _merge_kernel(a_ref, cb_ref, cc_ref, cv_ref, zc_ref, hc_ref, hv_ref, mq_ref, zm_ref,
                  g0a_ref, g0b_ref, g1a_ref, g1b_ref, g2a_ref, g2b_ref,
                  x_ref, mk_ref, mv_ref, cw_ref, mqn_ref,
                  wa_ref, wc_ref, wm_ref, wo_ref, o_ref, u_ref):
    ts = MERGE_TS
    tiles_per_seq = SEQ // ts
    seq_start = (pl.program_id(0) % tiles_per_seq) == 0

    u = cc_ref[...].astype(F32) * cv_ref[...].astype(F32)
    uh = hc_ref[...].astype(F32) * hv_ref[...].astype(F32)
    u_ref[0:HALO, :] = jnp.where(seq_start, 0.0, uh)
    u_ref[HALO:HALO + ts, :] = u
    y = (cw_ref[0:1, :] * u
         + cw_ref[1:2, :] * u_ref[HALO - 1:HALO - 1 + ts, :]
         + cw_ref[2:3, :] * u_ref[HALO - 2:HALO - 2 + ts, :])
    c = (cb_ref[...].astype(F32) * y * _silu(zc_ref[...].astype(F32))).astype(BF16)

    gq = mqn_ref[...] * (MEM_HEAD_DIM ** -0.5)
    mo_parts = []
    for h in range(MEM_HEADS):
        cols = slice(h * MEM_HEAD_DIM, (h + 1) * MEM_HEAD_DIM)
        qh = _rms(mq_ref[:, cols].astype(F32), gq).astype(BF16)
        s = lax.dot_general(qh, mk_ref[:, cols], (((1,), (1,)), ((), ())),
                            preferred_element_type=F32)
        p = jnp.exp(s - jnp.max(s, axis=-1, keepdims=True))
        den = jnp.sum(p, axis=-1, keepdims=True)
        oh = jnp.dot(p.astype(BF16), mv_ref[:, cols], preferred_element_type=F32) / den
        mo_parts.append((oh * _silu(zm_ref[:, cols].astype(F32))).astype(BF16))
    mo = jnp.concatenate(mo_parts, axis=-1)

    ya = jnp.dot(a_ref[...], wa_ref[...], preferred_element_type=F32)
    yc = jnp.dot(c, wc_ref[...], preferred_element_type=F32)
    ym = jnp.dot(mo, wm_ref[...], preferred_element_type=F32)

    half = D_MODEL // 2
    merged = []
    for part, (ga, gc, gm) in enumerate(((g0a_ref, g1a_ref, g2a_ref), (g0b_ref, g1b_ref, g2b_ref))):
        cols = slice(part * half, (part + 1) * half)
        merged.append((jax.nn.sigmoid(ga[...].astype(F32)) * ya[:, cols]
                       + jax.nn.sigmoid(gc[...].astype(F32)) * yc[:, cols]
                       + jax.nn.sigmoid(gm[...].astype(F32)) * ym[:, cols]).astype(BF16))
    merged = jnp.concatenate(merged, axis=-1)
    o_ref[...] = x_ref[...] + jnp.dot(merged, wo_ref[...], preferred_element_type=F32)


def _merge(a, proj, x2, mk, mv, conv_w, mqn, wa, wc, wm, wo):
    n = x2.shape[0]
    ts = MERGE_TS
    tiles_per_seq = SEQ // ts
    half = D_MODEL // 2

    def col_spec(col, width):
        return pl.BlockSpec((ts, width), lambda t: (t, col // width))

    def halo_spec(col):
        return pl.BlockSpec((HALO, CONV_WIDTH),
                            lambda t: (jnp.maximum(t * (ts // HALO) - 1, 0), col // CONV_WIDTH))

    def const_spec(shape):
        return pl.BlockSpec(shape, lambda t: (0,) * len(shape), pipeline_mode=pl.Buffered(1))

    mem_spec = pl.BlockSpec((None, MEM_LEN, MEM_W), lambda t: (t // tiles_per_seq, 0, 0))
    in_specs = [
        pl.BlockSpec((ts, ATTN_OUT), lambda t: (t, 0)),
        col_spec(COL_CB, CONV_WIDTH), col_spec(COL_CC, CONV_WIDTH),
        col_spec(COL_CV, CONV_WIDTH), col_spec(COL_ZC, CONV_WIDTH),
        halo_spec(COL_CC), halo_spec(COL_CV),
        col_spec(COL_MQ, MEM_W), col_spec(COL_ZM, MEM_W),
    ] + [col_spec(COL_G + k * half, half) for k in range(2 * N_BRANCH)] + [
        pl.BlockSpec((ts, D_MODEL), lambda t: (t, 0)),
        mem_spec, mem_spec,
        const_spec((CONV_K, CONV_WIDTH)), const_spec((1, MEM_HEAD_DIM)),
        const_spec((ATTN_OUT, D_MODEL)), const_spec((CONV_WIDTH, D_MODEL)),
        const_spec((MEM_W, D_MODEL)), const_spec((D_MODEL, D_MODEL)),
    ]
    n_proj = 8 + 2 * N_BRANCH
    return pl.pallas_call(
        _merge_kernel,
        grid=(n // ts,),
        in_specs=in_specs,
        out_specs=pl.BlockSpec((ts, D_MODEL), lambda t: (t, 0)),
        out_shape=jax.ShapeDtypeStruct((n, D_MODEL), F32),
        scratch_shapes=[pltpu.VMEM((HALO + ts, CONV_WIDTH), F32)],
        compiler_params=pltpu.CompilerParams(
            dimension_semantics=("arbitrary",),
            vmem_limit_bytes=VMEM_LIMIT_BYTES),
        name="merge",
    )(a, *([proj] * n_proj), x2, mk, mv, conv_w, mqn, wa, wc, wm, wo)


def kernel(x, mem, norm_g, mem_norm_g, w_in, attn_q_norm, attn_k_norm, conv_w, mem_w_kv,
           mem_q_norm, mem_k_norm, w_br_attn, w_br_conv, w_br_mem, w_out):
    b, s, d = x.shape
    assert (s, d) == (SEQ, D_MODEL) and w_in.shape == (D_MODEL, IN_COLS)
    x2 = x.reshape(b * s, d)
    proj = _in_proj(x2, norm_g.reshape(1, d), w_in.astype(BF16))
    mk, mv = _mem_kv(mem, mem_norm_g.reshape(1, d), mem_w_kv.astype(BF16),
                     mem_k_norm.reshape(1, MEM_HEAD_DIM))
    a = _attn(proj, attn_q_norm, attn_k_norm)
    out = _merge(a, proj, x2, mk, mv, conv_w, mem_q_norm.reshape(1, MEM_HEAD_DIM),
                 w_br_attn.astype(BF16), w_br_conv.astype(BF16), w_br_mem.astype(BF16),
                 w_out.astype(BF16))
    return out.reshape(b, s, d)
```

```python
import functools

import jax
import jax.numpy as jnp
from jax import lax
from jax.experimental import pallas as pl
from jax.experimental.pallas import tpu as pltpu

D_MODEL = 2048
SEQ = 2048
HEAD_DIM = 128
ATTN_PATTERNS = ((128, 1), (512, 4), (2048, 16))
N_GROUPS = len(ATTN_PATTERNS)
HEADS_PER_GROUP = 4
GROUP_W = HEADS_PER_GROUP * HEAD_DIM
ATTN_QKV = N_GROUPS * GROUP_W
ATTN_OUT = GROUP_W
BLK = 128
CONV_WIDTH = 1024
CONV_K = 3
MEM_LEN = 256
MEM_HEADS = 4
MEM_HEAD_DIM = 256
MEM_W = MEM_HEADS * MEM_HEAD_DIM
N_BRANCH = 3
EPS = 1e-6
IN_COLS = 3 * ATTN_QKV + ATTN_OUT + 4 * CONV_WIDTH + 2 * MEM_W + N_BRANCH * D_MODEL

COL_Q = 0
COL_K = ATTN_QKV
COL_V = 2 * ATTN_QKV
COL_ZA = 3 * ATTN_QKV
COL_CB = COL_ZA + ATTN_OUT
COL_CC = COL_CB + CONV_WIDTH
COL_CV = COL_CC + CONV_WIDTH
COL_ZC = COL_CV + CONV_WIDTH
COL_MQ = COL_ZC + CONV_WIDTH
COL_ZM = COL_MQ + MEM_W
COL_G = COL_ZM + MEM_W

F32 = jnp.float32
BF16 = jnp.bfloat16

VMEM_LIMIT_BYTES = 56 * 1024 * 1024

PROJ_TM = 1024
PROJ_TN = 1024
MERGE_TS = 256
HALO = 16
ROW_CHUNK = 256


def _rms(t, gain):
    return t * lax.rsqrt(jnp.mean(t * t, axis=-1, keepdims=True) + EPS) * gain


def _silu(z):
    return z * jax.nn.sigmoid(z)


def _in_proj_kernel(x_ref, g_ref, w_ref, o_ref, h_ref):
    @pl.when(pl.program_id(1) == 0)
    def _():
        def body(c, _):
            rows = pl.ds(pl.multiple_of(c * ROW_CHUNK, ROW_CHUNK), ROW_CHUNK)
            h_ref[rows, :] = _rms(x_ref[rows, :], g_ref[...]).astype(BF16)
            return _
        lax.fori_loop(0, PROJ_TM // ROW_CHUNK, body, None)

    o_ref[...] = jnp.dot(h_ref[...], w_ref[...], preferred_element_type=F32).astype(o_ref.dtype)


def _in_proj(x2, g, w):
    n = x2.shape[0]
    return pl.pallas_call(
        _in_proj_kernel,
        grid=(n // PROJ_TM, IN_COLS // PROJ_TN),
        in_specs=[
            pl.BlockSpec((PROJ_TM, D_MODEL), lambda i, j: (i, 0)),
            pl.BlockSpec((1, D_MODEL), lambda i, j: (0, 0)),
            pl.BlockSpec((D_MODEL, PROJ_TN), lambda i, j: (0, j)),
        ],
        out_specs=pl.BlockSpec((PROJ_TM, PROJ_TN), lambda i, j: (i, j)),
        out_shape=jax.ShapeDtypeStruct((n, IN_COLS), BF16),
        scratch_shapes=[pltpu.VMEM((PROJ_TM, D_MODEL), BF16)],
        compiler_params=pltpu.CompilerParams(
            dimension_semantics=("arbitrary", "arbitrary"),
            vmem_limit_bytes=VMEM_LIMIT_BYTES),
        name="in_proj",
    )(x2, g, w)


def _mem_kv_kernel(mem_ref, g_ref, w_ref, kn_ref, mk_ref, mv_ref):
    mh = _rms(mem_ref[...], g_ref[...]).astype(BF16)
    kv = jnp.dot(mh, w_ref[...], preferred_element_type=F32)
    for h in range(MEM_HEADS):
        cols = slice(h * MEM_HEAD_DIM, (h + 1) * MEM_HEAD_DIM)
        mk_ref[:, cols] = _rms(kv[:, cols], kn_ref[...]).astype(BF16)
    mv_ref[...] = kv[:, MEM_W:].astype(BF16)


def _mem_kv(mem, g, w, kn):
    b = mem.shape[0]
    return pl.pallas_call(
        _mem_kv_kernel,
        grid=(b,),
        in_specs=[
            pl.BlockSpec((None, MEM_LEN, D_MODEL), lambda i: (i, 0, 0)),
            pl.BlockSpec((1, D_MODEL), lambda i: (0, 0)),
            pl.BlockSpec((D_MODEL, 2 * MEM_W), lambda i: (0, 0)),
            pl.BlockSpec((1, MEM_HEAD_DIM), lambda i: (0, 0)),
        ],
        out_specs=[
            pl.BlockSpec((None, MEM_LEN, MEM_W), lambda i: (i, 0, 0)),
            pl.BlockSpec((None, MEM_LEN, MEM_W), lambda i: (i, 0, 0)),
        ],
        out_shape=[jax.ShapeDtypeStruct((b, MEM_LEN, MEM_W), BF16)] * 2,
        compiler_params=pltpu.CompilerParams(
            dimension_semantics=("arbitrary",),
            vmem_limit_bytes=VMEM_LIMIT_BYTES),
        name="mem_kv",
    )(mem, g, w, kn)


CLS = 4
CLS_LEN = SEQ // CLS
STAGE_ROWS = 256
STAGE_PIECE = STAGE_ROWS // CLS


def _band_biases(bias_ref):
    r = lax.broadcasted_iota(jnp.int32, (BLK, 2 * BLK), 0)
    c = lax.broadcasted_iota(jnp.int32, (BLK, 2 * BLK), 1)

    def put(idx, diff):
        in_band = lax.bitcast_convert_type(diff, jnp.uint32) <= jnp.uint32(BLK)
        bias_ref[idx] = jnp.where(in_band, 0.0, -jnp.inf)

    put(0, c - r)
    piece = BLK // CLS
    b, jq = r // piece, r % piece
    bk, jk = c // (2 * piece), c % (2 * piece)
    put(1, CLS * (piece + jq - jk) + (b - bk))
    bk, jk = c // piece, c % piece
    put(2, CLS * (jq - jk) + (b - bk))


def _load_rows(ref, h, pieces):
    parts = [ref[h, p, :] for p in pieces]
    return parts[0] if len(parts) == 1 else jnp.concatenate(parts, axis=0)


def _store_rows(ref, h, pieces, sizes, val):
    off = 0
    for p, n in zip(pieces, sizes):
        ref[h, p, :] = val[off:off + n]
        off += n


def _attn_block(qp, qs, kp, bias, first, sq_ref, sk_ref, sv_ref, acc_ref, m_ref, l_ref):
    nk = bias.shape[1]
    ones = jnp.ones((nk, HEAD_DIM), BF16)
    heads_per_dot = 2 * BLK // nk
    zeros = jnp.zeros((nk, HEAD_DIM), BF16)
    for h0 in range(0, HEADS_PER_GROUP, heads_per_dot):
        hs = range(h0, h0 + heads_per_dot)
        qb = jnp.concatenate([_load_rows(sq_ref, h, qp) for h in hs], axis=-1).astype(BF16)
        ks = [_load_rows(sk_ref, h, kp).astype(BF16) for h in hs]
        vs = [_load_rows(sv_ref, h, kp).astype(BF16) for h in hs]
        if heads_per_dot == 1:
            kb = ks[0]
            vb = jnp.concatenate([vs[0], ones], axis=-1)
            bias_w = bias
        else:
            kb = jnp.concatenate([jnp.concatenate([ks[0], zeros], axis=-1),
                                  jnp.concatenate([zeros, ks[1]], axis=-1)], axis=0)
            vb = jnp.concatenate([jnp.concatenate([vs[0], ones, zeros, zeros], axis=-1),
                                  jnp.concatenate([zeros, zeros, vs[1], ones], axis=-1)], axis=0)
            bias_w = jnp.concatenate([bias, bias], axis=-1)
        s = lax.dot_general(qb, kb, (((1,), (1,)), ((), ())), preferred_element_type=F32) + bias_w
        w = s.shape[1] // heads_per_dot
        m_olds, m_news = [], []
        for i, h in enumerate(hs):
            mb = jnp.max(s[:, i * w:(i + 1) * w], axis=-1, keepdims=True)
            if first:
                m_olds.append(None)
                m_news.append(jnp.broadcast_to(mb, (BLK, HEAD_DIM)))
            else:
                m_olds.append(_load_rows(m_ref, h, qp))
                m_news.append(jnp.maximum(m_olds[-1], mb))
        m_wide = jnp.concatenate([m for m in m_news for _ in range(w // HEAD_DIM)], axis=-1)
        p = jnp.exp(s - m_wide).astype(BF16)
        pv = jnp.dot(p, vb, preferred_element_type=F32)
        for i, h in enumerate(hs):
            acc_new = pv[:, 2 * i * HEAD_DIM:(2 * i + 1) * HEAD_DIM]
            l_new = pv[:, (2 * i + 1) * HEAD_DIM:(2 * i + 2) * HEAD_DIM]
            if not first:
                alpha = jnp.exp(m_olds[i] - m_news[i])
                acc_new = alpha * _load_rows(acc_ref, h, qp) + acc_new
                l_new = alpha * _load_rows(l_ref, h, qp) + l_new
            _store_rows(m_ref, h, qp, qs, m_news[i])
            _store_rows(l_ref, h, qp, qs, l_new)
            _store_rows(acc_ref, h, qp, qs, acc_new)


def _attn_group(grp, d, first, q_ref, k_ref, v_ref, qn_ref, kn_ref,
                bias_ref, tmp_ref, acc_ref, m_ref, l_ref, sq_ref, sk_ref, sv_ref):
    gq = qn_ref[grp:grp + 1, :] * (HEAD_DIM ** -0.5)
    gk = kn_ref[grp:grp + 1, :]

    def stage(c, _):
        rows = pl.ds(pl.multiple_of(c * STAGE_ROWS, STAGE_ROWS), STAGE_ROWS)
        for h in range(HEADS_PER_GROUP):
            cols = slice(h * HEAD_DIM, (h + 1) * HEAD_DIM)
            tmp_ref[0, h] = _rms(q_ref[rows, cols].astype(F32), gq)
            tmp_ref[1, h] = _rms(k_ref[rows, cols].astype(F32), gk)
            tmp_ref[2, h] = v_ref[rows, cols].astype(F32)
        for h in range(HEADS_PER_GROUP):
            for b in range(CLS):
                dst = pl.ds(pl.multiple_of(b * CLS_LEN + c * STAGE_PIECE, STAGE_PIECE), STAGE_PIECE)
                src = pl.ds(b, STAGE_PIECE, stride=CLS)
                sq_ref[h, dst, :] = tmp_ref[0, h, src, :]
                sk_ref[h, dst, :] = tmp_ref[1, h, src, :]
                sv_ref[h, dst, :] = tmp_ref[2, h, src, :]
        return _
    lax.fori_loop(0, SEQ // STAGE_ROWS, stage, None)

    block = functools.partial(
        _attn_block, first=first, sq_ref=sq_ref, sk_ref=sk_ref, sv_ref=sv_ref,
        acc_ref=acc_ref, m_ref=m_ref, l_ref=l_ref)
    band = bias_ref[0]
    causal = band[:, BLK:]
    if d == 16:
        for r in range(d):
            a, b = divmod(r, CLS)
            rows = [pl.ds(b * CLS_LEN + a, BLK, stride=CLS)]
            block(rows, [BLK], rows, bias_ref[0, :, BLK:])
    elif d == 4:
        for b in range(CLS):
            for bi in range(CLS_LEN // BLK):
                q0 = b * CLS_LEN + bi * BLK
                if bi == 0:
                    block([pl.ds(q0, BLK)], [BLK], [pl.ds(q0, BLK)], causal)
                else:
                    block([pl.ds(q0, BLK)], [BLK], [pl.ds(q0 - BLK, 2 * BLK)], band)
    else:
        piece = BLK // CLS
        for bi in range(SEQ // BLK):
            qp = [pl.ds(b * CLS_LEN + bi * piece, piece) for b in range(CLS)]
            if bi == 0:
                block(qp, [piece] * CLS, qp, bias_ref[2][:, :BLK])
            else:
                kp = [pl.ds(b * CLS_LEN + (bi - 1) * piece, 2 * piece) for b in range(CLS)]
                block(qp, [piece] * CLS, kp, bias_ref[1])


def _attn_kernel(q_ref, k_ref, v_ref, z_ref, qn_ref, kn_ref, o_ref,
                 bias_ref, tmp_ref, acc_ref, m_ref, l_ref, sq_ref, sk_ref, sv_ref):
    step = pl.program_id(1)
    _band_biases(bias_ref)

    for s in range(N_GROUPS):
        grp = N_GROUPS - 1 - s

        @pl.when(step == s)
        def _(s=s, grp=grp):
            _attn_group(grp, ATTN_PATTERNS[grp][1], s == 0, q_ref, k_ref, v_ref, qn_ref, kn_ref,
                        bias_ref, tmp_ref, acc_ref, m_ref, l_ref, sq_ref, sk_ref, sv_ref)

    @pl.when(step == N_GROUPS - 1)
    def _():
        half = CLS_LEN // 2
        for h in range(HEADS_PER_GROUP):
            for b in range(CLS):
                for k in range(2):
                    src = pl.ds(b * CLS_LEN + k * half, half)
                    sq_ref[h, pl.ds(b + k * half * CLS, half, stride=CLS), :] = (
                        acc_ref[h, src, :] / l_ref[h, src, :])

        def fin(c, _):
            rows = pl.ds(pl.multiple_of(c * ROW_CHUNK, ROW_CHUNK), ROW_CHUNK)
            for h in range(HEADS_PER_GROUP):
                cols = slice(h * HEAD_DIM, (h + 1) * HEAD_DIM)
                o_ref[rows, cols] = (sq_ref[h, rows, :]
                                     * _silu(z_ref[rows, cols].astype(F32))).astype(o_ref.dtype)
            return _
        lax.fori_loop(0, SEQ // ROW_CHUNK, fin, None)


def _attn(proj, qn, kn):
    n = proj.shape[0]
    b = n // SEQ
    qb, kb, vb, zb = (c // GROUP_W for c in (COL_Q, COL_K, COL_V, COL_ZA))

    def grp_spec(base):
        return pl.BlockSpec((SEQ, GROUP_W), lambda i, s: (i, base + N_GROUPS - 1 - s))

    slab = pltpu.VMEM((HEADS_PER_GROUP, SEQ, HEAD_DIM), F32)
    return pl.pallas_call(
        _attn_kernel,
        grid=(b, N_GROUPS),
        in_specs=[
            grp_spec(qb), grp_spec(kb), grp_spec(vb),
            pl.BlockSpec((SEQ, GROUP_W), lambda i, s: (i, zb)),
            pl.BlockSpec((N_GROUPS, HEAD_DIM), lambda i, s: (0, 0)),
            pl.BlockSpec((N_GROUPS, HEAD_DIM), lambda i, s: (0, 0)),
        ],
        out_specs=pl.BlockSpec((SEQ, ATTN_OUT), lambda i, s: (i, 0)),
        out_shape=jax.ShapeDtypeStruct((n, ATTN_OUT), BF16),
        scratch_shapes=[
            pltpu.VMEM((3, BLK, 2 * BLK), F32),
            pltpu.VMEM((3, HEADS_PER_GROUP, STAGE_ROWS, HEAD_DIM), F32),
        ] + [slab] * 6,
        compiler_params=pltpu.CompilerParams(
            dimension_semantics=("arbitrary", "arbitrary"),
            vmem_limit_bytes=VMEM_LIMIT_BYTES),
        name="attn",
    )(proj, proj, proj, proj, qn, kn)


def _merge_kernel(a_ref, cb_ref, cc_ref, cv_ref, zc_ref, hc_ref, hv_ref, mq_ref, zm_ref,
                  g0a_ref, g0b_ref, g1a_ref, g1b_ref, g2a_ref, g2b_ref,
                  x_ref, mk_ref, mv_ref, cw_ref, mqn_ref,
                  wa_ref, wc_ref, wm_ref, wo_ref, o_ref, u_ref):
    ts = MERGE_TS
    tiles_per_seq = SEQ // ts
    seq_start = (pl.program_id(0) % tiles_per_seq) == 0

    u = cc_ref[...].astype(F32) * cv_ref[...].astype(F32)
    uh = hc_ref[...].astype(F32) * hv_ref[...].astype(F32)
    u_ref[0:HALO, :] = jnp.where(seq_start, 0.0, uh)
    u_ref[HALO:HALO + ts, :] = u
    y = (cw_ref[0:1, :] * u
         + cw_ref[1:2, :] * u_ref[HALO - 1:HALO - 1 + ts, :]
         + cw_ref[2:3, :] * u_ref[HALO - 2:HALO - 2 + ts, :])
    c = (cb_ref[...].astype(F32) * y * _silu(zc_ref[...].astype(F32))).astype(BF16)

    gq = mqn_ref[...] * (MEM_HEAD_DIM ** -0.5)
    mo_parts = []
    for h in range(MEM_HEADS):
        cols = slice(h * MEM_HEAD_DIM, (h + 1) * MEM_HEAD_DIM)
        qh = _rms(mq_ref[:, cols].astype(F32), gq).astype(BF16)
        s = lax.dot_general(qh, mk_ref[:, cols], (((1,), (1,)), ((), ())),
                            preferred_element_type=F32)
        p = jnp.exp(s - jnp.max(s, axis=-1, keepdims=True))
        den = jnp.sum(p, axis=-1, keepdims=True)
        oh = jnp.dot(p.astype(BF16), mv_ref[:, cols], preferred_element_type=F32) / den
        mo_parts.append((oh * _silu(zm_ref[:, cols].astype(F32))).astype(BF16))
    mo = jnp.concatenate(mo_parts, axis=-1)

    ya = jnp.dot(a_ref[...], wa_ref[...], preferred_element_type=F32)
    yc = jnp.dot(c, wc_ref[...], preferred_element_type=F32)
    ym = jnp.dot(mo, wm_ref[...], preferred_element_type=F32)

    half = D_MODEL // 2
    merged = []
    for part, (ga, gc, gm) in enumerate(((g0a_ref, g1a_ref, g2a_ref), (g0b_ref, g1b_ref, g2b_ref))):
        cols = slice(part * half, (part + 1) * half)
        merged.append((jax.nn.sigmoid(ga[...].astype(F32)) * ya[:, cols]
                       + jax.nn.sigmoid(gc[...].astype(F32)) * yc[:, cols]
                       + jax.nn.sigmoid(gm[...].astype(F32)) * ym[:, cols]).astype(BF16))
    merged = jnp.concatenate(merged, axis=-1)
    o_ref[...] = x_ref[...] + jnp.dot(merged, wo_ref[...], preferred_element_type=F32)


def _merge(a, proj, x2, mk, mv, conv_w, mqn, wa, wc, wm, wo):
    n = x2.shape[0]
    ts = MERGE_TS
    tiles_per_seq = SEQ // ts
    half = D_MODEL // 2

    def col_spec(col, width):
        return pl.BlockSpec((ts, width), lambda t: (t, col // width))

    def halo_spec(col):
        return pl.BlockSpec((HALO, CONV_WIDTH),
                            lambda t: (jnp.maximum(t * (ts // HALO) - 1, 0), col // CONV_WIDTH))

    def const_spec(shape):
        return pl.BlockSpec(shape, lambda t: (0,) * len(shape), pipeline_mode=pl.Buffered(1))

    mem_spec = pl.BlockSpec((None, MEM_LEN, MEM_W), lambda t: (t // tiles_per_seq, 0, 0))
    in_specs = [
        pl.BlockSpec((ts, ATTN_OUT), lambda t: (t, 0)),
        col_spec(COL_CB, CONV_WIDTH), col_spec(COL_CC, CONV_WIDTH),
        col_spec(COL_CV, CONV_WIDTH), col_spec(COL_ZC, CONV_WIDTH),
        halo_spec(COL_CC), halo_spec(COL_CV),
        col_spec(COL_MQ, MEM_W), col_spec(COL_ZM, MEM_W),
    ] + [col_spec(COL_G + k * half, half) for k in range(2 * N_BRANCH)] + [
        pl.BlockSpec((ts, D_MODEL), lambda t: (t, 0)),
        mem_spec, mem_spec,
        const_spec((CONV_K, CONV_WIDTH)), const_spec((1, MEM_HEAD_DIM)),
        const_spec((ATTN_OUT, D_MODEL)), const_spec((CONV_WIDTH, D_MODEL)),
        const_spec((MEM_W, D_MODEL)), const_spec((D_MODEL, D_MODEL)),
    ]
    n_proj = 8 + 2 * N_BRANCH
    return pl.pallas_call(
        _merge_kernel,
        grid=(n // ts,),
        in_specs=in_specs,
        out_specs=pl.BlockSpec((ts, D_MODEL), lambda t: (t, 0)),
        out_shape=jax.ShapeDtypeStruct((n, D_MODEL), F32),
        scratch_shapes=[pltpu.VMEM((HALO + ts, CONV_WIDTH), F32)],
        compiler_params=pltpu.CompilerParams(
            dimension_semantics=("arbitrary",),
            vmem_limit_bytes=VMEM_LIMIT_BYTES),
        name="merge",
    )(a, *([proj] * n_proj), x2, mk, mv, conv_w, mqn, wa, wc, wm, wo)


def kernel(x, mem, norm_g, mem_norm_g, w_in, attn_q_norm, attn_k_norm, conv_w, mem_w_kv,
           mem_q_norm, mem_k_norm, w_br_attn, w_br_conv, w_br_mem, w_out):
    b, s, d = x.shape
    assert (s, d) == (SEQ, D_MODEL) and w_in.shape == (D_MODEL, IN_COLS)
    x2 = x.reshape(b * s, d)
    proj = _in_proj(x2, norm_g.reshape(1, d), w_in.astype(BF16))
    mk, mv = _mem_kv(mem, mem_norm_g.reshape(1, d), mem_w_kv.astype(BF16),
                     mem_k_norm.reshape(1, MEM_HEAD_DIM))
    a = _attn(proj, attn_q_norm, attn_k_norm)
    out = _merge(a, proj, x2, mk, mv, conv_w, mem_q_norm.reshape(1, MEM_HEAD_DIM),
                 w_br_attn.astype(BF16), w_br_conv.astype(BF16), w_br_mem.astype(BF16),
                 w_out.astype(BF16))
    return out.reshape(b, s, d)
```

```python
import functools

import jax
import jax.numpy as jnp
from jax import lax
from jax.experimental import pallas as pl
from jax.experimental.pallas import tpu as pltpu

D_MODEL = 2048
SEQ = 2048
HEAD_DIM = 128
ATTN_PATTERNS = ((128, 1), (512, 4), (2048, 16))
N_GROUPS = len(ATTN_PATTERNS)
HEADS_PER_GROUP = 4
GROUP_W = HEADS_PER_GROUP * HEAD_DIM
ATTN_QKV = N_GROUPS * GROUP_W
ATTN_OUT = GROUP_W
BLK = 128
CONV_WIDTH = 1024
CONV_K = 3
MEM_LEN = 256
MEM_HEADS = 4
MEM_HEAD_DIM = 256
MEM_W = MEM_HEADS * MEM_HEAD_DIM
N_BRANCH = 3
EPS = 1e-6
IN_COLS = 3 * ATTN_QKV + ATTN_OUT + 4 * CONV_WIDTH + 2 * MEM_W + N_BRANCH * D_MODEL

COL_Q = 0
COL_K = ATTN_QKV
COL_V = 2 * ATTN_QKV
COL_ZA = 3 * ATTN_QKV
COL_CB = COL_ZA + ATTN_OUT
COL_CC = COL_CB + CONV_WIDTH
COL_CV = COL_CC + CONV_WIDTH
COL_ZC = COL_CV + CONV_WIDTH
COL_MQ = COL_ZC + CONV_WIDTH
COL_ZM = COL_MQ + MEM_W
COL_G = COL_ZM + MEM_W

F32 = jnp.float32
BF16 = jnp.bfloat16

VMEM_LIMIT_BYTES = 56 * 1024 * 1024

PROJ_TM = 1024
PROJ_TN = 1024
MERGE_TS = 256
HALO = 16
ROW_CHUNK = 256


def _rms(t, gain):
    return t * lax.rsqrt(jnp.mean(t * t, axis=-1, keepdims=True) + EPS) * gain


def _silu(z):
    return z * jax.nn.sigmoid(z)


CLS = 4
CLS_LEN = SEQ // CLS
HEADS_PER_TILE = PROJ_TN // HEAD_DIM
N_QKV_HEADS = 3 * N_GROUPS * HEADS_PER_GROUP
N_NORM_HEADS = 2 * N_GROUPS * HEADS_PER_GROUP
N_HEAD_TILES = -(-N_QKV_HEADS // HEADS_PER_TILE)
N_FULL_HEAD_TILES = N_QKV_HEADS // HEADS_PER_TILE
N_NORM_TILES = N_NORM_HEADS // HEADS_PER_TILE
TAIL_HEADS = N_QKV_HEADS - N_FULL_HEAD_TILES * HEADS_PER_TILE
REST_COL0 = N_FULL_HEAD_TILES * PROJ_TN
REST_COLS = IN_COLS - REST_COL0
assert N_NORM_HEADS % HEADS_PER_TILE == 0 and SEQ % PROJ_TM == 0 and PROJ_TM % CLS == 0


def _in_proj_kernel(x_ref, g_ref, hg_ref, w_ref, qkv_ref, o_ref, h_ref, res_ref):
    j = pl.program_id(1)

    @pl.when(j == 0)
    def _():
        def body(c, _):
            rows = pl.ds(pl.multiple_of(c * ROW_CHUNK, ROW_CHUNK), ROW_CHUNK)
            h_ref[rows, :] = _rms(x_ref[rows, :], g_ref[...]).astype(BF16)
            return _
        lax.fori_loop(0, PROJ_TM // ROW_CHUNK, body, None)

    def project():
        return jnp.dot(h_ref[...], w_ref[...], preferred_element_type=F32)

    def emit_heads(res, n_heads, normed):
        for s in range(n_heads):
            r = res[:, s * HEAD_DIM:(s + 1) * HEAD_DIM]
            res_ref[s] = _rms(r, hg_ref[s:s + 1, :]) if normed else r
            for b in range(CLS):
                qkv_ref[s, b] = res_ref[s, pl.ds(b, PROJ_TM // CLS, stride=CLS), :]

    @pl.when(j < N_NORM_TILES)
    def _():
        emit_heads(project(), HEADS_PER_TILE, True)

    @pl.when((j >= N_NORM_TILES) & (j < N_FULL_HEAD_TILES))
    def _():
        emit_heads(project(), HEADS_PER_TILE, False)

    @pl.when(j == N_FULL_HEAD_TILES)
    def _():
        res = project()
        emit_heads(res, TAIL_HEADS, False)
        qkv_ref[TAIL_HEADS:] = jnp.zeros((HEADS_PER_TILE - TAIL_HEADS,) + qkv_ref.shape[1:], F32)
        o_ref[...] = res.astype(o_ref.dtype)

    @pl.when(j > N_FULL_HEAD_TILES)
    def _():
        o_ref[...] = project().astype(o_ref.dtype)


def _in_proj(x2, g, head_gains, w):
    n = x2.shape[0]
    tiles_per_seq = SEQ // PROJ_TM
    rows = PROJ_TM // CLS
    return pl.pallas_call(
        _in_proj_kernel,
        grid=(n // PROJ_TM, IN_COLS // PROJ_TN),
        in_specs=[
            pl.BlockSpec((PROJ_TM, D_MODEL), lambda i, j: (i, 0)),
            pl.BlockSpec((1, D_MODEL), lambda i, j: (0, 0)),
            pl.BlockSpec((HEADS_PER_TILE, HEAD_DIM),
                         lambda i, j: (jnp.minimum(j, N_NORM_TILES - 1), 0)),
            pl.BlockSpec((D_MODEL, PROJ_TN), lambda i, j: (0, j)),
        ],
        out_specs=[
            pl.BlockSpec((HEADS_PER_TILE, None, CLS, rows, HEAD_DIM),
                         lambda i, j: (jnp.minimum(j, N_FULL_HEAD_TILES), i // tiles_per_seq, 0,
                                       i % tiles_per_seq, 0)),
            pl.BlockSpec((PROJ_TM, PROJ_TN),
                         lambda i, j: (i, jnp.maximum(j, N_FULL_HEAD_TILES) - N_FULL_HEAD_TILES)),
        ],
        out_shape=[
            jax.ShapeDtypeStruct((N_HEAD_TILES * HEADS_PER_TILE, n // SEQ, CLS, CLS_LEN, HEAD_DIM), F32),
            jax.ShapeDtypeStruct((n, REST_COLS), BF16),
        ],
        scratch_shapes=[pltpu.VMEM((PROJ_TM, D_MODEL), BF16),
                        pltpu.VMEM((HEADS_PER_TILE, PROJ_TM, HEAD_DIM), F32)],
        compiler_params=pltpu.CompilerParams(
            dimension_semantics=("arbitrary", "arbitrary"),
            vmem_limit_bytes=VMEM_LIMIT_BYTES),
        name="in_proj",
    )(x2, g, head_gains, w)


def _mem_kv_kernel(mem_ref, g_ref, w_ref, kn_ref, mk_ref, mv_ref):
    mh = _rms(mem_ref[...], g_ref[...]).astype(BF16)
    kv = jnp.dot(mh, w_ref[...], preferred_element_type=F32)
    for h in range(MEM_HEADS):
        cols = slice(h * MEM_HEAD_DIM, (h + 1) * MEM_HEAD_DIM)
        mk_ref[:, cols] = _rms(kv[:, cols], kn_ref[...]).astype(BF16)
    mv_ref[...] = kv[:, MEM_W:].astype(BF16)


def _mem_kv(mem, g, w, kn):
    b = mem.shape[0]
    return pl.pallas_call(
        _mem_kv_kernel,
        grid=(b,),
        in_specs=[
            pl.BlockSpec((None, MEM_LEN, D_MODEL), lambda i: (i, 0, 0)),
            pl.BlockSpec((1, D_MODEL), lambda i: (0, 0)),
            pl.BlockSpec((D_MODEL, 2 * MEM_W), lambda i: (0, 0)),
            pl.BlockSpec((1, MEM_HEAD_DIM), lambda i: (0, 0)),
        ],
        out_specs=[
            pl.BlockSpec((None, MEM_LEN, MEM_W), lambda i: (i, 0, 0)),
            pl.BlockSpec((None, MEM_LEN, MEM_W), lambda i: (i, 0, 0)),
        ],
        out_shape=[jax.ShapeDtypeStruct((b, MEM_LEN, MEM_W), BF16)] * 2,
        compiler_params=pltpu.CompilerParams(
            dimension_semantics=("arbitrary",),
            vmem_limit_bytes=VMEM_LIMIT_BYTES),
        name="mem_kv",
    )(mem, g, w, kn)


def _band_biases(bias_ref):
    r = lax.broadcasted_iota(jnp.int32, (BLK, 2 * BLK), 0)
    c = lax.broadcasted_iota(jnp.int32, (BLK, 2 * BLK), 1)

    def put(idx, diff):
        in_band = lax.bitcast_convert_type(diff, jnp.uint32) <= jnp.uint32(BLK)
        bias_ref[idx] = jnp.where(in_band, 0.0, -jnp.inf)

    put(0, c - r)
    piece = BLK // CLS
    b, jq = r // piece, r % piece
    bk, jk = c // (2 * piece), c % (2 * piece)
    put(1, CLS * (piece + jq - jk) + (b - bk))
    bk, jk = c // piece, c % piece
    put(2, CLS * (jq - jk) + (b - bk))


def _load_rows(ref, h, pieces):
    parts = [ref[h, b, rows, :] for b, rows in pieces]
    return parts[0] if len(parts) == 1 else jnp.concatenate(parts, axis=0)


def _store_rows(ref, h, pieces, val):
    off = 0
    for b, rows in pieces:
        ref[h, b, rows, :] = val[off:off + rows.size]
        off += rows.size


def _attn_block(qp, kp, bias, first, q_ref, k_ref, v_ref, acc_ref, m_ref, l_ref):
    nk = bias.shape[1]
    ones = jnp.ones((nk, HEAD_DIM), BF16)
    heads_per_dot = 2 * BLK // nk
    zeros = jnp.zeros((nk, HEAD_DIM), BF16)
    for h0 in range(0, HEADS_PER_GROUP, heads_per_dot):
        hs = range(h0, h0 + heads_per_dot)
        qb = jnp.concatenate([_load_rows(q_ref, h, qp) for h in hs], axis=-1).astype(BF16)
        ks = [_load_rows(k_ref, h, kp).astype(BF16) for h in hs]
        vs = [_load_rows(v_ref, h, kp).astype(BF16) for h in hs]
        if heads_per_dot == 1:
            kb = ks[0]
            vb = jnp.concatenate([vs[0], ones], axis=-1)
            bias_w = bias
        else:
            kb = jnp.concatenate([jnp.concatenate([ks[0], zeros], axis=-1),
                                  jnp.concatenate([zeros, ks[1]], axis=-1)], axis=0)
            vb = jnp.concatenate([jnp.concatenate([vs[0], ones, zeros, zeros], axis=-1),
                                  jnp.concatenate([zeros, zeros, vs[1], ones], axis=-1)], axis=0)
            bias_w = jnp.concatenate([bias, bias], axis=-1)
        s = lax.dot_general(qb, kb, (((1,), (1,)), ((), ())), preferred_element_type=F32) + bias_w
        w = s.shape[1] // heads_per_dot
        m_olds, m_news = [], []
        for i, h in enumerate(hs):
            mb = jnp.max(s[:, i * w:(i + 1) * w], axis=-1, keepdims=True)
            if first:
                m_olds.append(None)
                m_news.append(jnp.broadcast_to(mb, (BLK, HEAD_DIM)))
            else:
                m_olds.append(_load_rows(m_ref, h, qp))
                m_news.append(jnp.maximum(m_olds[-1], mb))
        m_wide = jnp.concatenate([m for m in m_news for _ in range(w // HEAD_DIM)], axis=-1)
        p = jnp.exp(s - m_wide).astype(BF16)
        pv = jnp.dot(p, vb, preferred_element_type=F32)
        for i, h in enumerate(hs):
            acc_new = pv[:, 2 * i * HEAD_DIM:(2 * i + 1) * HEAD_DIM]
            l_new = pv[:, (2 * i + 1) * HEAD_DIM:(2 * i + 2) * HEAD_DIM]
            if not first:
                alpha = jnp.exp(m_olds[i] - m_news[i])
                acc_new = alpha * _load_rows(acc_ref, h, qp) + acc_new
                l_new = alpha * _load_rows(l_ref, h, qp) + l_new
            _store_rows(m_ref, h, qp, m_news[i])
            _store_rows(l_ref, h, qp, l_new)
            _store_rows(acc_ref, h, qp, acc_new)


def _attn_group(d, first, q_ref, k_ref, v_ref, bias_ref, acc_ref, m_ref, l_ref):
    block = functools.partial(_attn_block, first=first, q_ref=q_ref, k_ref=k_ref, v_ref=v_ref,
                              acc_ref=acc_ref, m_ref=m_ref, l_ref=l_ref)
    if d == 16:
        for r in range(d):
            a, b = divmod(r, CLS)
            rows = [(b, pl.ds(a, BLK, stride=CLS))]
            block(rows, rows, bias_ref[0, :, BLK:])
    elif d == 4:
        for b in range(CLS):
            for bi in range(CLS_LEN // BLK):
                q0 = bi * BLK
                if bi == 0:
                    block([(b, pl.ds(q0, BLK))], [(b, pl.ds(q0, BLK))], bias_ref[0, :, BLK:])
                else:
                    block([(b, pl.ds(q0, BLK))], [(b, pl.ds(q0 - BLK, 2 * BLK))], bias_ref[0])
    else:
        piece = BLK // CLS
        for bi in range(SEQ // BLK):
            qp = [(b, pl.ds(bi * piece, piece)) for b in range(CLS)]
            if bi == 0:
                block(qp, qp, bias_ref[2, :, :BLK])
            else:
                kp = [(b, pl.ds((bi - 1) * piece, 2 * piece)) for b in range(CLS)]
                block(qp, kp, bias_ref[1])


def _attn_kernel(q_ref, k_ref, v_ref, z_ref, o_ref, bias_ref, acc_ref, m_ref, l_ref, nat_ref):
    step = pl.program_id(1)
    _band_biases(bias_ref)

    for s in range(N_GROUPS):
        @pl.when(step == s)
        def _(s=s):
            _attn_group(ATTN_PATTERNS[N_GROUPS - 1 - s][1], s == 0, q_ref, k_ref, v_ref,
                        bias_ref, acc_ref, m_ref, l_ref)

    @pl.when(step == N_GROUPS - 1)
    def _():
        half = CLS_LEN // 2
        for h in range(HEADS_PER_GROUP):
            for b in range(CLS):
                for k in range(2):
                    src = pl.ds(k * half, half)
                    nat_ref[h, pl.ds(b + k * half * CLS, half, stride=CLS), :] = (
                        acc_ref[h, b, src, :] / l_ref[h, b, src, :])

        def fin(c, _):
            rows = pl.ds(pl.multiple_of(c * ROW_CHUNK, ROW_CHUNK), ROW_CHUNK)
            for h in range(HEADS_PER_GROUP):
                cols = slice(h * HEAD_DIM, (h + 1) * HEAD_DIM)
                o_ref[rows, cols] = (nat_ref[h, rows, :]
                                     * _silu(z_ref[rows, cols].astype(F32))).astype(o_ref.dtype)
            return _
        lax.fori_loop(0, SEQ // ROW_CHUNK, fin, None)


def _attn(qkv, rest):
    nb = qkv.shape[1]
    n = nb * SEQ

    def head_spec(kind):
        return pl.BlockSpec((HEADS_PER_GROUP, None, CLS, CLS_LEN, HEAD_DIM),
                            lambda i, s: (kind * N_GROUPS + N_GROUPS - 1 - s, i, 0, 0, 0))

    slab = pltpu.VMEM((HEADS_PER_GROUP, CLS, CLS_LEN, HEAD_DIM), F32)
    return pl.pallas_call(
        _attn_kernel,
        grid=(nb, N_GROUPS),
        in_specs=[
            head_spec(0), head_spec(1), head_spec(2),
            pl.BlockSpec((SEQ, ATTN_OUT), lambda i, s: (i, (COL_ZA - REST_COL0) // ATTN_OUT)),
        ],
        out_specs=pl.BlockSpec((SEQ, ATTN_OUT), lambda i, s: (i, 0)),
        out_shape=jax.ShapeDtypeStruct((n, ATTN_OUT), BF16),
        scratch_shapes=[pltpu.VMEM((3, BLK, 2 * BLK), F32), slab, slab, slab,
                        pltpu.VMEM((HEADS_PER_GROUP, SEQ, HEAD_DIM), F32)],
        compiler_params=pltpu.CompilerParams(
            dimension_semantics=("arbitrary", "arbitrary"),
            vmem_limit_bytes=VMEM_LIMIT_BYTES),
        name="attn",
    )(qkv, qkv, qkv, rest)


def _merge_kernel(a_ref, cb_ref, cc_ref, cv_ref, zc_ref, hc_ref, hv_ref, mq_ref, zm_ref,
                  g0a_ref, g0b_ref, g1a_ref, g1b_ref, g2a_ref, g2b_ref,
                  x_ref, mk_ref, mv_ref, cw_ref, mqn_ref,
                  wa_ref, wc_ref, wm_ref, wo_ref, o_ref, u_ref, mg_ref, mgp_ref, *, n_tiles):
    ts = MERGE_TS
    tiles_per_seq = SEQ // ts
    t = pl.program_id(0)
    seq_start = (jnp.minimum(t, n_tiles - 1) % tiles_per_seq) == 0

    @pl.when(t == 0)
    def _():
        mg_ref[...] = jnp.zeros_like(mg_ref)

    mgp_ref[...] = mg_ref[...]
    o_ref[...] = x_ref[...] + jnp.dot(mgp_ref[...], wo_ref[...], preferred_element_type=F32)

    u = cc_ref[...].astype(F32) * cv_ref[...].astype(F32)
    uh = hc_ref[...].astype(F32) * hv_ref[...].astype(F32)
    u_ref[0:HALO, :] = jnp.where(seq_start, 0.0, uh)
    u_ref[HALO:HALO + ts, :] = u
    y = (cw_ref[0:1, :] * u
         + cw_ref[1:2, :] * u_ref[HALO - 1:HALO - 1 + ts, :]
         + cw_ref[2:3, :] * u_ref[HALO - 2:HALO - 2 + ts, :])
    c = (cb_ref[...].astype(F32) * y * _silu(zc_ref[...].astype(F32))).astype(BF16)

    gq = mqn_ref[...] * (MEM_HEAD_DIM ** -0.5)
    mo_parts = []
    for h in range(MEM_HEADS):
        cols = slice(h * MEM_HEAD_DIM, (h + 1) * MEM_HEAD_DIM)
        qh = _rms(mq_ref[:, cols].astype(F32), gq).astype(BF16)
        s = lax.dot_general(qh, mk_ref[:, cols], (((1,), (1,)), ((), ())),
                            preferred_element_type=F32)
        p = jnp.exp(s - jnp.max(s, axis=-1, keepdims=True))
        den = jnp.sum(p, axis=-1, keepdims=True)
        oh = jnp.dot(p.astype(BF16), mv_ref[:, cols], preferred_element_type=F32) / den
        mo_parts.append((oh * _silu(zm_ref[:, cols].astype(F32))).astype(BF16))
    mo = jnp.concatenate(mo_parts, axis=-1)

    ya = jnp.dot(a_ref[...], wa_ref[...], preferred_element_type=F32)
    yc = jnp.dot(c, wc_ref[...], preferred_element_type=F32)
    ym = jnp.dot(mo, wm_ref[...], preferred_element_type=F32)

    half = D_MODEL // 2
    for part, (ga, gc, gm) in enumerate(((g0a_ref, g1a_ref, g2a_ref), (g0b_ref, g1b_ref, g2b_ref))):
        cols = slice(part * half, (part + 1) * half)
        mg_ref[:, cols] = (jax.nn.sigmoid(ga[...].astype(F32)) * ya[:, cols]
                           + jax.nn.sigmoid(gc[...].astype(F32)) * yc[:, cols]
                           + jax.nn.sigmoid(gm[...].astype(F32)) * ym[:, cols]).astype(BF16)


def _merge(a, rest, x2, mk, mv, conv_w, mqn, wa, wc, wm, wo):
    n = x2.shape[0]
    ts = MERGE_TS
    n_tiles = n // ts
    tiles_per_seq = SEQ // ts
    half = D_MODEL // 2

    def cur(t):
        return jnp.minimum(t, n_tiles - 1)

    def prev(t):
        return jnp.maximum(t - 1, 0)

    def col_spec(col, width):
        return pl.BlockSpec((ts, width), lambda t: (cur(t), (col - REST_COL0) // width))

    def halo_spec(col):
        return pl.BlockSpec((HALO, CONV_WIDTH),
                            lambda t: (jnp.maximum(cur(t) * (ts // HALO) - 1, 0),
                                       (col - REST_COL0) // CONV_WIDTH))

    def const_spec(shape):
        return pl.BlockSpec(shape, lambda t: (0,) * len(shape), pipeline_mode=pl.Buffered(1))

    mem_spec = pl.BlockSpec((None, MEM_LEN, MEM_W), lambda t: (cur(t) // tiles_per_seq, 0, 0))
    in_specs = [
        pl.BlockSpec((ts, ATTN_OUT), lambda t: (cur(t), 0)),
        col_spec(COL_CB, CONV_WIDTH), col_spec(COL_CC, CONV_WIDTH),
        col_spec(COL_CV, CONV_WIDTH), col_spec(COL_ZC, CONV_WIDTH),
        halo_spec(COL_CC), halo_spec(COL_CV),
        col_spec(COL_MQ, MEM_W), col_spec(COL_ZM, MEM_W),
    ] + [col_spec(COL_G + k * half, half) for k in range(2 * N_BRANCH)] + [
        pl.BlockSpec((ts, D_MODEL), lambda t: (prev(t), 0)),
        mem_spec, mem_spec,
        const_spec((CONV_K, CONV_WIDTH)), const_spec((1, MEM_HEAD_DIM)),
        const_spec((ATTN_OUT, D_MODEL)), const_spec((CONV_WIDTH, D_MODEL)),
        const_spec((MEM_W, D_MODEL)), const_spec((D_MODEL, D_MODEL)),
    ]
    n_rest = 8 + 2 * N_BRANCH
    return pl.pallas_call(
        functools.partial(_merge_kernel, n_tiles=n_tiles),
        grid=(n_tiles + 1,),
        in_specs=in_specs,
        out_specs=pl.BlockSpec((ts, D_MODEL), lambda t: (prev(t), 0)),
        out_shape=jax.ShapeDtypeStruct((n, D_MODEL), F32),
        scratch_shapes=[pltpu.VMEM((HALO + ts, CONV_WIDTH), F32),
                        pltpu.VMEM((ts, D_MODEL), BF16), pltpu.VMEM((ts, D_MODEL), BF16)],
        compiler_params=pltpu.CompilerParams(
            dimension_semantics=("arbitrary",),
            vmem_limit_bytes=VMEM_LIMIT_BYTES),
        name="merge",
    )(a, *([rest] * n_rest), x2, mk, mv, conv_w, mqn, wa, wc, wm, wo)


def kernel(x, mem, norm_g, mem_norm_g, w_in, attn_q_norm, attn_k_norm, conv_w, mem_w_kv,
           mem_q_norm, mem_k_norm, w_br_attn, w_br_conv, w_br_mem, w_out):
    b, s, d = x.shape
    assert (s, d) == (SEQ, D_MODEL) and w_in.shape == (D_MODEL, IN_COLS)
    x2 = x.reshape(b * s, d)
    scale = HEAD_DIM ** -0.5
    head_gains = jnp.concatenate([jnp.repeat(attn_q_norm * scale, HEADS_PER_GROUP, axis=0),
                                  jnp.repeat(attn_k_norm, HEADS_PER_GROUP, axis=0)], axis=0)
    qkv, rest = _in_proj(x2, norm_g.reshape(1, d), head_gains, w_in.astype(BF16))
    mk, mv = _mem_kv(mem, mem_norm_g.reshape(1, d), mem_w_kv.astype(BF16),
                     mem_k_norm.reshape(1, MEM_HEAD_DIM))
    a = _attn(qkv, rest)
    out = _merge(a, rest, x2, mk, mv, conv_w, mem_q_norm.reshape(1, MEM_HEAD_DIM),
                 w_br_attn.astype(BF16), w_br_conv.astype(BF16), w_br_mem.astype(BF16),
                 w_out.astype(BF16))
    return out.reshape(b, s, d)
```

```python
import functools

import jax
import jax.numpy as jnp
from jax import lax
from jax.experimental import pallas as pl
from jax.experimental.pallas import tpu as pltpu

D_MODEL = 2048
SEQ = 2048
HEAD_DIM = 128
ATTN_PATTERNS = ((128, 1), (512, 4), (2048, 16))
N_GROUPS = len(ATTN_PATTERNS)
HEADS_PER_GROUP = 4
GROUP_W = HEADS_PER_GROUP * HEAD_DIM
ATTN_QKV = N_GROUPS * GROUP_W
ATTN_OUT = GROUP_W
BLK = 128
CONV_WIDTH = 1024
CONV_K = 3
MEM_LEN = 256
MEM_HEADS = 4
MEM_HEAD_DIM = 256
MEM_W = MEM_HEADS * MEM_HEAD_DIM
N_BRANCH = 3
EPS = 1e-6
IN_COLS = 3 * ATTN_QKV + ATTN_OUT + 4 * CONV_WIDTH + 2 * MEM_W + N_BRANCH * D_MODEL

COL_Q = 0
COL_K = ATTN_QKV
COL_V = 2 * ATTN_QKV
COL_ZA = 3 * ATTN_QKV
COL_CB = COL_ZA + ATTN_OUT
COL_CC = COL_CB + CONV_WIDTH
COL_CV = COL_CC + CONV_WIDTH
COL_ZC = COL_CV + CONV_WIDTH
COL_MQ = COL_ZC + CONV_WIDTH
COL_ZM = COL_MQ + MEM_W
COL_G = COL_ZM + MEM_W

F32 = jnp.float32
BF16 = jnp.bfloat16

VMEM_LIMIT_BYTES = 56 * 1024 * 1024

PROJ_TM = 1024
PROJ_TN = 1024
MERGE_TS = 256
HALO = 16
ROW_CHUNK = 256


def _rms(t, gain):
    return t * lax.rsqrt(jnp.mean(t * t, axis=-1, keepdims=True) + EPS) * gain


def _silu(z):
    return z * jax.nn.sigmoid(z)


CLS = 4
CLS_LEN = SEQ // CLS
HEADS_PER_TILE = PROJ_TN // HEAD_DIM
N_QKV_HEADS = 3 * N_GROUPS * HEADS_PER_GROUP
N_NORM_HEADS = 2 * N_GROUPS * HEADS_PER_GROUP
N_HEAD_TILES = -(-N_QKV_HEADS // HEADS_PER_TILE)
N_FULL_HEAD_TILES = N_QKV_HEADS // HEADS_PER_TILE
N_NORM_TILES = N_NORM_HEADS // HEADS_PER_TILE
TAIL_HEADS = N_QKV_HEADS - N_FULL_HEAD_TILES * HEADS_PER_TILE
REST_COL0 = N_HEAD_TILES * PROJ_TN
REST_COLS = IN_COLS - REST_COL0
assert N_NORM_HEADS % HEADS_PER_TILE == 0 and SEQ % PROJ_TM == 0 and PROJ_TM % CLS == 0
assert COL_ZA == N_QKV_HEADS * HEAD_DIM and COL_ZA + ATTN_OUT == REST_COL0


def _in_proj_kernel(x_ref, g_ref, hg_ref, w_ref, qkv_ref, za_ref, o_ref, h_ref, hp_ref, tmp_ref):
    j = pl.program_id(1)
    rows_per_cls = PROJ_TM // CLS

    @pl.when(j == 0)
    def _():
        def body(c, _):
            rows = pl.ds(pl.multiple_of(c * ROW_CHUNK, ROW_CHUNK), ROW_CHUNK)
            hf = _rms(x_ref[rows, :], g_ref[...])
            h_ref[rows, :] = hf.astype(BF16)
            piece = ROW_CHUNK // CLS
            for s in range(D_MODEL // HEAD_DIM):
                lanes = slice(s * HEAD_DIM, (s + 1) * HEAD_DIM)
                tmp_ref[s] = hf[:, lanes]
                for b in range(CLS):
                    dst = pl.ds(pl.multiple_of(b * rows_per_cls + c * piece, piece), piece)
                    hp_ref[dst, lanes] = tmp_ref[s, pl.ds(b, piece, stride=CLS), :].astype(BF16)
            return _
        lax.fori_loop(0, PROJ_TM // ROW_CHUNK, body, None)

    def emit_heads(n_heads, normed):
        res = jnp.dot(hp_ref[...], w_ref[...], preferred_element_type=F32)
        for s in range(n_heads):
            r = res[:, s * HEAD_DIM:(s + 1) * HEAD_DIM]
            if normed:
                r = _rms(r, hg_ref[s:s + 1, :])
            for b in range(CLS):
                qkv_ref[s, b] = r[b * rows_per_cls:(b + 1) * rows_per_cls]
        return res

    @pl.when(j < N_NORM_TILES)
    def _():
        emit_heads(HEADS_PER_TILE, True)

    @pl.when((j >= N_NORM_TILES) & (j < N_FULL_HEAD_TILES))
    def _():
        emit_heads(HEADS_PER_TILE, False)

    @pl.when(j == N_FULL_HEAD_TILES)
    def _():
        res = emit_heads(TAIL_HEADS, False)
        qkv_ref[TAIL_HEADS:] = jnp.zeros((HEADS_PER_TILE - TAIL_HEADS,) + qkv_ref.shape[1:], F32)
        for b in range(CLS):
            za_ref[b] = res[b * rows_per_cls:(b + 1) * rows_per_cls,
                            TAIL_HEADS * HEAD_DIM:].astype(za_ref.dtype)

    @pl.when(j > N_FULL_HEAD_TILES)
    def _():
        o_ref[...] = jnp.dot(h_ref[...], w_ref[...], preferred_element_type=F32).astype(o_ref.dtype)


def _in_proj(x2, g, head_gains, w):
    n = x2.shape[0]
    tiles_per_seq = SEQ // PROJ_TM
    rows = PROJ_TM // CLS
    return pl.pallas_call(
        _in_proj_kernel,
        grid=(n // PROJ_TM, IN_COLS // PROJ_TN),
        in_specs=[
            pl.BlockSpec((PROJ_TM, D_MODEL), lambda i, j: (i, 0)),
            pl.BlockSpec((1, D_MODEL), lambda i, j: (0, 0)),
            pl.BlockSpec((HEADS_PER_TILE, HEAD_DIM),
                         lambda i, j: (jnp.minimum(j, N_NORM_TILES - 1), 0)),
            pl.BlockSpec((D_MODEL, PROJ_TN), lambda i, j: (0, j)),
        ],
        out_specs=[
            pl.BlockSpec((HEADS_PER_TILE, None, CLS, rows, HEAD_DIM),
                         lambda i, j: (jnp.minimum(j, N_FULL_HEAD_TILES), i // tiles_per_seq, 0,
                                       i % tiles_per_seq, 0)),
            pl.BlockSpec((None, CLS, rows, ATTN_OUT),
                         lambda i, j: (i // tiles_per_seq, 0, i % tiles_per_seq, 0)),
            pl.BlockSpec((PROJ_TM, PROJ_TN),
                         lambda i, j: (i, jnp.maximum(j, N_HEAD_TILES) - N_HEAD_TILES)),
        ],
        out_shape=[
            jax.ShapeDtypeStruct((N_HEAD_TILES * HEADS_PER_TILE, n // SEQ, CLS, CLS_LEN, HEAD_DIM), F32),
            jax.ShapeDtypeStruct((n // SEQ, CLS, CLS_LEN, ATTN_OUT), BF16),
            jax.ShapeDtypeStruct((n, REST_COLS), BF16),
        ],
        scratch_shapes=[pltpu.VMEM((PROJ_TM, D_MODEL), BF16), pltpu.VMEM((PROJ_TM, D_MODEL), BF16),
                        pltpu.VMEM((D_MODEL // HEAD_DIM, ROW_CHUNK, HEAD_DIM), F32)],
        compiler_params=pltpu.CompilerParams(
            dimension_semantics=("arbitrary", "arbitrary"),
            vmem_limit_bytes=VMEM_LIMIT_BYTES),
        name="in_proj",
    )(x2, g, head_gains, w)


def _mem_kv_kernel(mem_ref, g_ref, w_ref, kn_ref, mk_ref, mv_ref):
    mh = _rms(mem_ref[...], g_ref[...]).astype(BF16)
    kv = jnp.dot(mh, w_ref[...], preferred_element_type=F32)
    for h in range(MEM_HEADS):
        cols = slice(h * MEM_HEAD_DIM, (h + 1) * MEM_HEAD_DIM)
        mk_ref[:, cols] = _rms(kv[:, cols], kn_ref[...]).astype(BF16)
    mv_ref[...] = kv[:, MEM_W:].astype(BF16)


def _mem_kv(mem, g, w, kn):
    b = mem.shape[0]
    return pl.pallas_call(
        _mem_kv_kernel,
        grid=(b,),
        in_specs=[
            pl.BlockSpec((None, MEM_LEN, D_MODEL), lambda i: (i, 0, 0)),
            pl.BlockSpec((1, D_MODEL), lambda i: (0, 0)),
            pl.BlockSpec((D_MODEL, 2 * MEM_W), lambda i: (0, 0)),
            pl.BlockSpec((1, MEM_HEAD_DIM), lambda i: (0, 0)),
        ],
        out_specs=[
            pl.BlockSpec((None, MEM_LEN, MEM_W), lambda i: (i, 0, 0)),
            pl.BlockSpec((None, MEM_LEN, MEM_W), lambda i: (i, 0, 0)),
        ],
        out_shape=[jax.ShapeDtypeStruct((b, MEM_LEN, MEM_W), BF16)] * 2,
        compiler_params=pltpu.CompilerParams(
            dimension_semantics=("arbitrary",),
            vmem_limit_bytes=VMEM_LIMIT_BYTES),
        name="mem_kv",
    )(mem, g, w, kn)


def _band_biases(bias_ref):
    r = lax.broadcasted_iota(jnp.int32, (BLK, 2 * BLK), 0)
    c = lax.broadcasted_iota(jnp.int32, (BLK, 2 * BLK), 1)

    def put(idx, diff):
        in_band = lax.bitcast_convert_type(diff, jnp.uint32) <= jnp.uint32(BLK)
        bias_ref[idx] = jnp.where(in_band, 0.0, -jnp.inf)

    put(0, c - r)
    piece = BLK // CLS
    b, jq = r // piece, r % piece
    bk, jk = c // (2 * piece), c % (2 * piece)
    put(1, CLS * (piece + jq - jk) + (b - bk))
    bk, jk = c // piece, c % piece
    put(2, CLS * (jq - jk) + (b - bk))


def _load_rows(ref, h, pieces):
    parts = [ref[h, b, rows, :] for b, rows in pieces]
    return parts[0] if len(parts) == 1 else jnp.concatenate(parts, axis=0)


def _store_rows(ref, h, pieces, val):
    off = 0
    for b, rows in pieces:
        ref[h, b, rows, :] = val[off:off + rows.size]
        off += rows.size


def _attn_block(qp, kp, bias, first, q_ref, k_ref, v_ref, acc_ref, m_ref, l_ref):
    nk = bias.shape[1]
    ones = jnp.ones((nk, HEAD_DIM), BF16)
    heads_per_dot = 2 * BLK // nk
    zeros = jnp.zeros((nk, HEAD_DIM), BF16)
    for h0 in range(0, HEADS_PER_GROUP, heads_per_dot):
        hs = range(h0, h0 + heads_per_dot)
        qb = jnp.concatenate([_load_rows(q_ref, h, qp) for h in hs], axis=-1).astype(BF16)
        ks = [_load_rows(k_ref, h, kp).astype(BF16) for h in hs]
        vs = [_load_rows(v_ref, h, kp).astype(BF16) for h in hs]
        if heads_per_dot == 1:
            kb = ks[0]
            vb = jnp.concatenate([vs[0], ones], axis=-1)
            bias_w = bias
        else:
            kb = jnp.concatenate([jnp.concatenate([ks[0], zeros], axis=-1),
                                  jnp.concatenate([zeros, ks[1]], axis=-1)], axis=0)
            vb = jnp.concatenate([jnp.concatenate([vs[0], ones, zeros, zeros], axis=-1),
                                  jnp.concatenate([zeros, zeros, vs[1], ones], axis=-1)], axis=0)
            bias_w = jnp.concatenate([bias, bias], axis=-1)
        s = lax.dot_general(qb, kb, (((1,), (1,)), ((), ())), preferred_element_type=F32) + bias_w
        w = s.shape[1] // heads_per_dot
        m_olds, m_news = [], []
        for i, h in enumerate(hs):
            mb = jnp.max(s[:, i * w:(i + 1) * w], axis=-1, keepdims=True)
            if first:
                m_olds.append(None)
                m_news.append(jnp.broadcast_to(mb, (BLK, HEAD_DIM)))
            else:
                m_olds.append(_load_rows(m_ref, h, qp))
                m_news.append(jnp.maximum(m_olds[-1], mb))
        m_wide = jnp.concatenate([m for m in m_news for _ in range(w // HEAD_DIM)], axis=-1)
        p = jnp.exp(s - m_wide).astype(BF16)
        pv = jnp.dot(p, vb, preferred_element_type=F32)
        for i, h in enumerate(hs):
            acc_new = pv[:, 2 * i * HEAD_DIM:(2 * i + 1) * HEAD_DIM]
            l_new = pv[:, (2 * i + 1) * HEAD_DIM:(2 * i + 2) * HEAD_DIM]
            if not first:
                alpha = jnp.exp(m_olds[i] - m_news[i])
                acc_new = alpha * _load_rows(acc_ref, h, qp) + acc_new
                l_new = alpha * _load_rows(l_ref, h, qp) + l_new
            _store_rows(m_ref, h, qp, m_news[i])
            _store_rows(l_ref, h, qp, l_new)
            _store_rows(acc_ref, h, qp, acc_new)


def _attn_group(d, first, q_ref, k_ref, v_ref, bias_ref, acc_ref, m_ref, l_ref):
    block = functools.partial(_attn_block, first=first, q_ref=q_ref, k_ref=k_ref, v_ref=v_ref,
                              acc_ref=acc_ref, m_ref=m_ref, l_ref=l_ref)
    if d == 16:
        for r in range(d):
            a, b = divmod(r, CLS)
            rows = [(b, pl.ds(a, BLK, stride=CLS))]
            block(rows, rows, bias_ref[0, :, BLK:])
    elif d == 4:
        for b in range(CLS):
            for bi in range(CLS_LEN // BLK):
                q0 = bi * BLK
                if bi == 0:
                    block([(b, pl.ds(q0, BLK))], [(b, pl.ds(q0, BLK))], bias_ref[0, :, BLK:])
                else:
                    block([(b, pl.ds(q0, BLK))], [(b, pl.ds(q0 - BLK, 2 * BLK))], bias_ref[0])
    else:
        piece = BLK // CLS
        for bi in range(SEQ // BLK):
            qp = [(b, pl.ds(bi * piece, piece)) for b in range(CLS)]
            if bi == 0:
                block(qp, qp, bias_ref[2, :, :BLK])
            else:
                kp = [(b, pl.ds((bi - 1) * piece, 2 * piece)) for b in range(CLS)]
                block(qp, kp, bias_ref[1])


def _attn_kernel(q_ref, k_ref, v_ref, z_ref, o_ref, bias_ref, acc_ref, m_ref, l_ref, nat_ref):
    step = pl.program_id(1)
    _band_biases(bias_ref)

    for s in range(N_GROUPS):
        @pl.when(step == s)
        def _(s=s):
            _attn_group(ATTN_PATTERNS[N_GROUPS - 1 - s][1], s == 0, q_ref, k_ref, v_ref,
                        bias_ref, acc_ref, m_ref, l_ref)

    @pl.when(step == N_GROUPS - 1)
    def _():
        half = CLS_LEN // 2
        for h in range(HEADS_PER_GROUP):
            cols = slice(h * HEAD_DIM, (h + 1) * HEAD_DIM)
            for b in range(CLS):
                for k in range(2):
                    src = pl.ds(k * half, half)
                    nat_ref[h, pl.ds(b + k * half * CLS, half, stride=CLS), :] = (
                        acc_ref[h, b, src, :] / l_ref[h, b, src, :]
                        * _silu(z_ref[b, src, cols].astype(F32)))

        def fin(c, _):
            rows = pl.ds(pl.multiple_of(c * ROW_CHUNK, ROW_CHUNK), ROW_CHUNK)
            for h in range(HEADS_PER_GROUP):
                o_ref[rows, h * HEAD_DIM:(h + 1) * HEAD_DIM] = nat_ref[h, rows, :].astype(o_ref.dtype)
            return _
        lax.fori_loop(0, SEQ // ROW_CHUNK, fin, None)


def _attn(qkv, zatt):
    nb = qkv.shape[1]
    n = nb * SEQ

    def head_spec(kind):
        return pl.BlockSpec((HEADS_PER_GROUP, None, CLS, CLS_LEN, HEAD_DIM),
                            lambda i, s: (kind * N_GROUPS + N_GROUPS - 1 - s, i, 0, 0, 0))

    slab = pltpu.VMEM((HEADS_PER_GROUP, CLS, CLS_LEN, HEAD_DIM), F32)
    return pl.pallas_call(
        _attn_kernel,
        grid=(nb, N_GROUPS),
        in_specs=[
            head_spec(0), head_spec(1), head_spec(2),
            pl.BlockSpec((None, CLS, CLS_LEN, ATTN_OUT), lambda i, s: (i, 0, 0, 0)),
        ],
        out_specs=pl.BlockSpec((SEQ, ATTN_OUT), lambda i, s: (i, 0)),
        out_shape=jax.ShapeDtypeStruct((n, ATTN_OUT), BF16),
        scratch_shapes=[pltpu.VMEM((3, BLK, 2 * BLK), F32), slab, slab, slab,
                        pltpu.VMEM((HEADS_PER_GROUP, SEQ, HEAD_DIM), F32)],
        compiler_params=pltpu.CompilerParams(
            dimension_semantics=("arbitrary", "arbitrary"),
            vmem_limit_bytes=VMEM_LIMIT_BYTES),
        name="attn",
    )(qkv, qkv, qkv, zatt)


def _merge_kernel(a_ref, cb_ref, cc_ref, cv_ref, zc_ref, hc_ref, hv_ref, mq_ref, zm_ref,
                  g0a_ref, g0b_ref, g1a_ref, g1b_ref, g2a_ref, g2b_ref,
                  x_ref, mk_ref, mv_ref, cw_ref, mqn_ref,
                  wa_ref, wc_ref, wm_ref, wo_ref, o_ref, u_ref, mg_ref, mgp_ref, *, n_tiles):
    ts = MERGE_TS
    tiles_per_seq = SEQ // ts
    t = pl.program_id(0)
    seq_start = (jnp.minimum(t, n_tiles - 1) % tiles_per_seq) == 0

    @pl.when(t == 0)
    def _():
        mg_ref[...] = jnp.zeros_like(mg_ref)

    mgp_ref[...] = mg_ref[...]
    o_ref[...] = x_ref[...] + jnp.dot(mgp_ref[...], wo_ref[...], preferred_element_type=F32)

    u = cc_ref[...].astype(F32) * cv_ref[...].astype(F32)
    uh = hc_ref[...].astype(F32) * hv_ref[...].astype(F32)
    u_ref[0:HALO, :] = jnp.where(seq_start, 0.0, uh)
    u_ref[HALO:HALO + ts, :] = u
    y = (cw_ref[0:1, :] * u
         + cw_ref[1:2, :] * u_ref[HALO - 1:HALO - 1 + ts, :]
         + cw_ref[2:3, :] * u_ref[HALO - 2:HALO - 2 + ts, :])
    c = (cb_ref[...].astype(F32) * y * _silu(zc_ref[...].astype(F32))).astype(BF16)

    gq = mqn_ref[...] * (MEM_HEAD_DIM ** -0.5)
    mo_parts = []
    for h in range(MEM_HEADS):
        cols = slice(h * MEM_HEAD_DIM, (h + 1) * MEM_HEAD_DIM)
        qh = _rms(mq_ref[:, cols].astype(F32), gq).astype(BF16)
        s = lax.dot_general(qh, mk_ref[:, cols], (((1,), (1,)), ((), ())),
                            preferred_element_type=F32)
        p = jnp.exp(s - jnp.max(s, axis=-1, keepdims=True))
        den = jnp.sum(p, axis=-1, keepdims=True)
        oh = jnp.dot(p.astype(BF16), mv_ref[:, cols], preferred_element_type=F32) / den
        mo_parts.append((oh * _silu(zm_ref[:, cols].astype(F32))).astype(BF16))
    mo = jnp.concatenate(mo_parts, axis=-1)

    ya = jnp.dot(a_ref[...], wa_ref[...], preferred_element_type=F32)
    yc = jnp.dot(c, wc_ref[...], preferred_element_type=F32)
    ym = jnp.dot(mo, wm_ref[...], preferred_element_type=F32)

    half = D_MODEL // 2
    for part, (ga, gc, gm) in enumerate(((g0a_ref, g1a_ref, g2a_ref), (g0b_ref, g1b_ref, g2b_ref))):
        cols = slice(part * half, (part + 1) * half)
        mg_ref[:, cols] = (jax.nn.sigmoid(ga[...].astype(F32)) * ya[:, cols]
                           + jax.nn.sigmoid(gc[...].astype(F32)) * yc[:, cols]
                           + jax.nn.sigmoid(gm[...].astype(F32)) * ym[:, cols]).astype(BF16)


def _merge(a, rest, x2, mk, mv, conv_w, mqn, wa, wc, wm, wo):
    n = x2.shape[0]
    ts = MERGE_TS
    n_tiles = n // ts
    tiles_per_seq = SEQ // ts
    half = D_MODEL // 2

    def cur(t):
        return jnp.minimum(t, n_tiles - 1)

    def prev(t):
        return jnp.maximum(t - 1, 0)

    def col_spec(col, width):
        return pl.BlockSpec((ts, width), lambda t: (cur(t), (col - REST_COL0) // width))

    def halo_spec(col):
        return pl.BlockSpec((HALO, CONV_WIDTH),
                            lambda t: (jnp.maximum(cur(t) * (ts // HALO) - 1, 0),
                                       (col - REST_COL0) // CONV_WIDTH))

    def const_spec(shape):
        return pl.BlockSpec(shape, lambda t: (0,) * len(shape), pipeline_mode=pl.Buffered(1))

    mem_spec = pl.BlockSpec((None, MEM_LEN, MEM_W), lambda t: (cur(t) // tiles_per_seq, 0, 0))
    in_specs = [
        pl.BlockSpec((ts, ATTN_OUT), lambda t: (cur(t), 0)),
        col_spec(COL_CB, CONV_WIDTH), col_spec(COL_CC, CONV_WIDTH),
        col_spec(COL_CV, CONV_WIDTH), col_spec(COL_ZC, CONV_WIDTH),
        halo_spec(COL_CC), halo_spec(COL_CV),
        col_spec(COL_MQ, MEM_W), col_spec(COL_ZM, MEM_W),
    ] + [col_spec(COL_G + k * half, half) for k in range(2 * N_BRANCH)] + [
        pl.BlockSpec((ts, D_MODEL), lambda t: (prev(t), 0)),
        mem_spec, mem_spec,
        const_spec((CONV_K, CONV_WIDTH)), const_spec((1, MEM_HEAD_DIM)),
        const_spec((ATTN_OUT, D_MODEL)), const_spec((CONV_WIDTH, D_MODEL)),
        const_spec((MEM_W, D_MODEL)), const_spec((D_MODEL, D_MODEL)),
    ]
    n_rest = 8 + 2 * N_BRANCH
    return pl.pallas_call(
        functools.partial(_merge_kernel, n_tiles=n_tiles),
        grid=(n_tiles + 1,),
        in_specs=in_specs,
        out_specs=pl.BlockSpec((ts, D_MODEL), lambda t: (prev(t), 0)),
        out_shape=jax.ShapeDtypeStruct((n, D_MODEL), F32),
        scratch_shapes=[pltpu.VMEM((HALO + ts, CONV_WIDTH), F32),
                        pltpu.VMEM((ts, D_MODEL), BF16), pltpu.VMEM((ts, D_MODEL), BF16)],
        compiler_params=pltpu.CompilerParams(
            dimension_semantics=("arbitrary",),
            vmem_limit_bytes=VMEM_LIMIT_BYTES),
        name="merge",
    )(a, *([rest] * n_rest), x2, mk, mv, conv_w, mqn, wa, wc, wm, wo)


def kernel(x, mem, norm_g, mem_norm_g, w_in, attn_q_norm, attn_k_norm, conv_w, mem_w_kv,
           mem_q_norm, mem_k_norm, w_br_attn, w_br_conv, w_br_mem, w_out):
    b, s, d = x.shape
    assert (s, d) == (SEQ, D_MODEL) and w_in.shape == (D_MODEL, IN_COLS)
    x2 = x.reshape(b * s, d)
    scale = HEAD_DIM ** -0.5
    head_gains = jnp.concatenate([jnp.repeat(attn_q_norm * scale, HEADS_PER_GROUP, axis=0),
                                  jnp.repeat(attn_k_norm, HEADS_PER_GROUP, axis=0)], axis=0)
    qkv, zatt, rest = _in_proj(x2, norm_g.reshape(1, d), head_gains, w_in.astype(BF16))
    mk, mv = _mem_kv(mem, mem_norm_g.reshape(1, d), mem_w_kv.astype(BF16),
                     mem_k_norm.reshape(1, MEM_HEAD_DIM))
    a = _attn(qkv, zatt)
    out = _merge(a, rest, x2, mk, mv, conv_w, mem_q_norm.reshape(1, MEM_HEAD_DIM),
                 w_br_attn.astype(BF16), w_br_conv.astype(BF16), w_br_mem.astype(BF16),
                 w_out.astype(BF16))
    return out.reshape(b, s, d)
```

```python
import functools

import jax
import jax.numpy as jnp
from jax import lax
from jax.experimental import pallas as pl
from jax.experimental.pallas import tpu as pltpu

D_MODEL = 2048
SEQ = 2048
HEAD_DIM = 128
ATTN_PATTERNS = ((128, 1), (512, 4), (2048, 16))
N_GROUPS = len(ATTN_PATTERNS)
HEADS_PER_GROUP = 4
GROUP_W = HEADS_PER_GROUP * HEAD_DIM
ATTN_QKV = N_GROUPS * GROUP_W
ATTN_OUT = GROUP_W
BLK = 128
CONV_WIDTH = 1024
CONV_K = 3
MEM_LEN = 256
MEM_HEADS = 4
MEM_HEAD_DIM = 256
MEM_W = MEM_HEADS * MEM_HEAD_DIM
N_BRANCH = 3
EPS = 1e-6
IN_COLS = 3 * ATTN_QKV + ATTN_OUT + 4 * CONV_WIDTH + 2 * MEM_W + N_BRANCH * D_MODEL

COL_Q = 0
COL_K = ATTN_QKV
COL_V = 2 * ATTN_QKV
COL_ZA = 3 * ATTN_QKV
COL_CB = COL_ZA + ATTN_OUT
COL_CC = COL_CB + CONV_WIDTH
COL_CV = COL_CC + CONV_WIDTH
COL_ZC = COL_CV + CONV_WIDTH
COL_MQ = COL_ZC + CONV_WIDTH
COL_ZM = COL_MQ + MEM_W
COL_G = COL_ZM + MEM_W

F32 = jnp.float32
BF16 = jnp.bfloat16

VMEM_LIMIT_BYTES = 56 * 1024 * 1024

PROJ_TM = 1024
PROJ_TN = 1024
MERGE_TS = 256
HALO = 16
ROW_CHUNK = 256


def _rms(t, gain):
    return t * lax.rsqrt(jnp.mean(t * t, axis=-1, keepdims=True) + EPS) * gain


def _silu(z):
    return z * jax.nn.sigmoid(z)


CLS = 4
CLS_LEN = SEQ // CLS
HEADS_PER_TILE = PROJ_TN // HEAD_DIM
N_QKV_HEADS = 3 * N_GROUPS * HEADS_PER_GROUP
N_NORM_HEADS = 2 * N_GROUPS * HEADS_PER_GROUP
N_HEAD_TILES = (COL_ZA + ATTN_OUT) // PROJ_TN
N_NORM_TILES = N_NORM_HEADS // HEADS_PER_TILE
REST_COL0 = N_HEAD_TILES * PROJ_TN
REST_COLS = IN_COLS - REST_COL0
assert N_NORM_HEADS % HEADS_PER_TILE == 0 and SEQ % PROJ_TM == 0 and PROJ_TM % CLS == 0
assert COL_ZA == N_QKV_HEADS * HEAD_DIM and (COL_ZA + ATTN_OUT) % PROJ_TN == 0


def _in_proj_kernel(x_ref, g_ref, hg_ref, w_ref, qkv_ref, o_ref, h_ref, hp_ref, tmp_ref):
    j = pl.program_id(1)
    rows_per_cls = PROJ_TM // CLS

    @pl.when(j == 0)
    def _():
        def body(c, _):
            rows = pl.ds(pl.multiple_of(c * ROW_CHUNK, ROW_CHUNK), ROW_CHUNK)
            hf = _rms(x_ref[rows, :], g_ref[...])
            h_ref[rows, :] = hf.astype(BF16)
            piece = ROW_CHUNK // CLS
            for s in range(D_MODEL // HEAD_DIM):
                lanes = slice(s * HEAD_DIM, (s + 1) * HEAD_DIM)
                tmp_ref[s] = hf[:, lanes]
                for b in range(CLS):
                    dst = pl.ds(pl.multiple_of(b * rows_per_cls + c * piece, piece), piece)
                    hp_ref[dst, lanes] = tmp_ref[s, pl.ds(b, piece, stride=CLS), :].astype(BF16)
            return _
        lax.fori_loop(0, PROJ_TM // ROW_CHUNK, body, None)

    def emit_heads(normed):
        res = jnp.dot(hp_ref[...], w_ref[...], preferred_element_type=F32)
        gains = hg_ref[pl.ds(pl.multiple_of(j * HEADS_PER_TILE, HEADS_PER_TILE), HEADS_PER_TILE), :]
        for s in range(HEADS_PER_TILE):
            r = res[:, s * HEAD_DIM:(s + 1) * HEAD_DIM]
            if normed:
                r = _rms(r, gains[s:s + 1, :])
            for b in range(CLS):
                qkv_ref[s, b] = r[b * rows_per_cls:(b + 1) * rows_per_cls]

    @pl.when(j < N_NORM_TILES)
    def _():
        emit_heads(True)

    @pl.when((j >= N_NORM_TILES) & (j < N_HEAD_TILES))
    def _():
        emit_heads(False)

    @pl.when(j >= N_HEAD_TILES)
    def _():
        o_ref[...] = jnp.dot(h_ref[...], w_ref[...], preferred_element_type=F32).astype(o_ref.dtype)


def _in_proj(x2, g, head_gains, w):
    n = x2.shape[0]
    tiles_per_seq = SEQ // PROJ_TM
    rows = PROJ_TM // CLS
    return pl.pallas_call(
        _in_proj_kernel,
        grid=(n // PROJ_TM, IN_COLS // PROJ_TN),
        in_specs=[
            pl.BlockSpec((PROJ_TM, D_MODEL), lambda i, j: (i, 0)),
            pl.BlockSpec((1, D_MODEL), lambda i, j: (0, 0)),
            pl.BlockSpec((N_HEAD_TILES * HEADS_PER_TILE, HEAD_DIM), lambda i, j: (0, 0)),
            pl.BlockSpec((D_MODEL, PROJ_TN), lambda i, j: (0, j)),
        ],
        out_specs=[
            pl.BlockSpec((HEADS_PER_TILE, None, CLS, rows, HEAD_DIM),
                         lambda i, j: (jnp.minimum(j, N_HEAD_TILES - 1), i // tiles_per_seq, 0,
                                       i % tiles_per_seq, 0)),
            pl.BlockSpec((PROJ_TM, PROJ_TN),
                         lambda i, j: (i, jnp.maximum(j, N_HEAD_TILES) - N_HEAD_TILES)),
        ],
        out_shape=[
            jax.ShapeDtypeStruct((N_HEAD_TILES * HEADS_PER_TILE, n // SEQ, CLS, CLS_LEN, HEAD_DIM), F32),
            jax.ShapeDtypeStruct((n, REST_COLS), BF16),
        ],
        scratch_shapes=[pltpu.VMEM((PROJ_TM, D_MODEL), BF16), pltpu.VMEM((PROJ_TM, D_MODEL), BF16),
                        pltpu.VMEM((D_MODEL // HEAD_DIM, ROW_CHUNK, HEAD_DIM), F32)],
        compiler_params=pltpu.CompilerParams(
            dimension_semantics=("arbitrary", "arbitrary"),
            vmem_limit_bytes=VMEM_LIMIT_BYTES),
        name="in_proj",
    )(x2, g, head_gains, w)


def _mem_kv_kernel(mem_ref, g_ref, w_ref, kn_ref, mk_ref, mv_ref):
    mh = _rms(mem_ref[...], g_ref[...]).astype(BF16)
    kv = jnp.dot(mh, w_ref[...], preferred_element_type=F32)
    for h in range(MEM_HEADS):
        cols = slice(h * MEM_HEAD_DIM, (h + 1) * MEM_HEAD_DIM)
        mk_ref[:, cols] = _rms(kv[:, cols], kn_ref[...]).astype(BF16)
    mv_ref[...] = kv[:, MEM_W:].astype(BF16)


def _mem_kv(mem, g, w, kn):
    b = mem.shape[0]
    return pl.pallas_call(
        _mem_kv_kernel,
        grid=(b,),
        in_specs=[
            pl.BlockSpec((None, MEM_LEN, D_MODEL), lambda i: (i, 0, 0)),
            pl.BlockSpec((1, D_MODEL), lambda i: (0, 0)),
            pl.BlockSpec((D_MODEL, 2 * MEM_W), lambda i: (0, 0)),
            pl.BlockSpec((1, MEM_HEAD_DIM), lambda i: (0, 0)),
        ],
        out_specs=[
            pl.BlockSpec((None, MEM_LEN, MEM_W), lambda i: (i, 0, 0)),
            pl.BlockSpec((None, MEM_LEN, MEM_W), lambda i: (i, 0, 0)),
        ],
        out_shape=[jax.ShapeDtypeStruct((b, MEM_LEN, MEM_W), BF16)] * 2,
        compiler_params=pltpu.CompilerParams(
            dimension_semantics=("arbitrary",),
            vmem_limit_bytes=VMEM_LIMIT_BYTES),
        name="mem_kv",
    )(mem, g, w, kn)


def _band_biases(bias_ref):
    r = lax.broadcasted_iota(jnp.int32, (BLK, 2 * BLK), 0)
    c = lax.broadcasted_iota(jnp.int32, (BLK, 2 * BLK), 1)

    def put(idx, diff):
        in_band = lax.bitcast_convert_type(diff, jnp.uint32) <= jnp.uint32(BLK)
        bias_ref[idx] = jnp.where(in_band, 0.0, -jnp.inf)

    put(0, c - r)
    piece = BLK // CLS
    b, jq = r // piece, r % piece
    bk, jk = c // (2 * piece), c % (2 * piece)
    put(1, CLS * (piece + jq - jk) + (b - bk))
    bk, jk = c // piece, c % piece
    put(2, CLS * (jq - jk) + (b - bk))


def _load_rows(ref, h, pieces):
    parts = [ref[h, b, rows, :] for b, rows in pieces]
    return parts[0] if len(parts) == 1 else jnp.concatenate(parts, axis=0)


def _store_rows(ref, h, pieces, val):
    off = 0
    for b, rows in pieces:
        ref[h, b, rows, :] = val[off:off + rows.size]
        off += rows.size


def _attn_block(qp, kp, bias, first, q_ref, k_ref, v_ref, acc_ref, m_ref, l_ref):
    nk = bias.shape[1]
    ones = jnp.ones((nk, HEAD_DIM), BF16)
    heads_per_dot = 2 * BLK // nk
    zeros = jnp.zeros((nk, HEAD_DIM), BF16)
    for h0 in range(0, HEADS_PER_GROUP, heads_per_dot):
        hs = range(h0, h0 + heads_per_dot)
        qb = jnp.concatenate([_load_rows(q_ref, h, qp) for h in hs], axis=-1).astype(BF16)
        ks = [_load_rows(k_ref, h, kp).astype(BF16) for h in hs]
        vs = [_load_rows(v_ref, h, kp).astype(BF16) for h in hs]
        if heads_per_dot == 1:
            kb = ks[0]
            vb = jnp.concatenate([vs[0], ones], axis=-1)
            bias_w = bias
        else:
            kb = jnp.concatenate([jnp.concatenate([ks[0], zeros], axis=-1),
                                  jnp.concatenate([zeros, ks[1]], axis=-1)], axis=0)
            vb = jnp.concatenate([jnp.concatenate([vs[0], ones, zeros, zeros], axis=-1),
                                  jnp.concatenate([zeros, zeros, vs[1], ones], axis=-1)], axis=0)
            bias_w = jnp.concatenate([bias, bias], axis=-1)
        s = lax.dot_general(qb, kb, (((1,), (1,)), ((), ())), preferred_element_type=F32) + bias_w
        w = s.shape[1] // heads_per_dot
        m_olds, m_news = [], []
        for i, h in enumerate(hs):
            mb = jnp.max(s[:, i * w:(i + 1) * w], axis=-1, keepdims=True)
            if first:
                m_olds.append(None)
                m_news.append(jnp.broadcast_to(mb, (BLK, HEAD_DIM)))
            else:
                m_olds.append(_load_rows(m_ref, h, qp))
                m_news.append(jnp.maximum(m_olds[-1], mb))
        m_wide = jnp.concatenate([m for m in m_news for _ in range(w // HEAD_DIM)], axis=-1)
        p = jnp.exp(s - m_wide).astype(BF16)
        pv = jnp.dot(p, vb, preferred_element_type=F32)
        for i, h in enumerate(hs):
            acc_new = pv[:, 2 * i * HEAD_DIM:(2 * i + 1) * HEAD_DIM]
            l_new = pv[:, (2 * i + 1) * HEAD_DIM:(2 * i + 2) * HEAD_DIM]
            if not first:
                alpha = jnp.exp(m_olds[i] - m_news[i])
                acc_new = alpha * _load_rows(acc_ref, h, qp) + acc_new
                l_new = alpha * _load_rows(l_ref, h, qp) + l_new
            _store_rows(m_ref, h, qp, m_news[i])
            _store_rows(l_ref, h, qp, l_new)
            _store_rows(acc_ref, h, qp, acc_new)


def _attn_group(d, first, q_ref, k_ref, v_ref, bias_ref, acc_ref, m_ref, l_ref):
    block = functools.partial(_attn_block, first=first, q_ref=q_ref, k_ref=k_ref, v_ref=v_ref,
                              acc_ref=acc_ref, m_ref=m_ref, l_ref=l_ref)
    if d == 16:
        for r in range(d):
            a, b = divmod(r, CLS)
            rows = [(b, pl.ds(a, BLK, stride=CLS))]
            block(rows, rows, bias_ref[0, :, BLK:])
    elif d == 4:
        for b in range(CLS):
            for bi in range(CLS_LEN // BLK):
                q0 = bi * BLK
                if bi == 0:
                    block([(b, pl.ds(q0, BLK))], [(b, pl.ds(q0, BLK))], bias_ref[0, :, BLK:])
                else:
                    block([(b, pl.ds(q0, BLK))], [(b, pl.ds(q0 - BLK, 2 * BLK))], bias_ref[0])
    else:
        piece = BLK // CLS
        for bi in range(SEQ // BLK):
            qp = [(b, pl.ds(bi * piece, piece)) for b in range(CLS)]
            if bi == 0:
                block(qp, qp, bias_ref[2, :, :BLK])
            else:
                kp = [(b, pl.ds((bi - 1) * piece, 2 * piece)) for b in range(CLS)]
                block(qp, kp, bias_ref[1])


def _attn_kernel(q_ref, k_ref, v_ref, z_ref, o_ref, bias_ref, acc_ref, m_ref, l_ref, nat_ref):
    step = pl.program_id(1)
    _band_biases(bias_ref)

    for s in range(N_GROUPS):
        @pl.when(step == s)
        def _(s=s):
            _attn_group(ATTN_PATTERNS[N_GROUPS - 1 - s][1], s == 0, q_ref, k_ref, v_ref,
                        bias_ref, acc_ref, m_ref, l_ref)

    @pl.when(step == N_GROUPS - 1)
    def _():
        half = CLS_LEN // 2
        for h in range(HEADS_PER_GROUP):
            for b in range(CLS):
                for k in range(2):
                    src = pl.ds(k * half, half)
                    nat_ref[h, pl.ds(b + k * half * CLS, half, stride=CLS), :] = (
                        acc_ref[h, b, src, :] / l_ref[h, b, src, :]
                        * _silu(z_ref[h, b, src, :]))

        def fin(c, _):
            rows = pl.ds(pl.multiple_of(c * ROW_CHUNK, ROW_CHUNK), ROW_CHUNK)
            for h in range(HEADS_PER_GROUP):
                o_ref[rows, h * HEAD_DIM:(h + 1) * HEAD_DIM] = nat_ref[h, rows, :].astype(o_ref.dtype)
            return _
        lax.fori_loop(0, SEQ // ROW_CHUNK, fin, None)


def _attn(qkv):
    nb = qkv.shape[1]
    n = nb * SEQ

    def head_spec(kind):
        return pl.BlockSpec((HEADS_PER_GROUP, None, CLS, CLS_LEN, HEAD_DIM),
                            lambda i, s: (kind * N_GROUPS + N_GROUPS - 1 - s, i, 0, 0, 0))

    slab = pltpu.VMEM((HEADS_PER_GROUP, CLS, CLS_LEN, HEAD_DIM), F32)
    return pl.pallas_call(
        _attn_kernel,
        grid=(nb, N_GROUPS),
        in_specs=[
            head_spec(0), head_spec(1), head_spec(2),
            pl.BlockSpec((HEADS_PER_GROUP, None, CLS, CLS_LEN, HEAD_DIM),
                         lambda i, s: (N_QKV_HEADS // HEADS_PER_GROUP, i, 0, 0, 0)),
        ],
        out_specs=pl.BlockSpec((SEQ, ATTN_OUT), lambda i, s: (i, 0)),
        out_shape=jax.ShapeDtypeStruct((n, ATTN_OUT), BF16),
        scratch_shapes=[pltpu.VMEM((3, BLK, 2 * BLK), F32), slab, slab, slab,
                        pltpu.VMEM((HEADS_PER_GROUP, SEQ, HEAD_DIM), F32)],
        compiler_params=pltpu.CompilerParams(
            dimension_semantics=("arbitrary", "arbitrary"),
            vmem_limit_bytes=VMEM_LIMIT_BYTES),
        name="attn",
    )(qkv, qkv, qkv, qkv)


def _merge_kernel(a_ref, cb_ref, cc_ref, cv_ref, zc_ref, hc_ref, hv_ref, mq_ref, zm_ref,
                  g0a_ref, g0b_ref, g1a_ref, g1b_ref, g2a_ref, g2b_ref,
                  x_ref, mk_ref, mv_ref, cw_ref, mqn_ref,
                  wa_ref, wc_ref, wm_ref, wo_ref, o_ref, u_ref, mg_ref, mgp_ref, *, n_tiles):
    ts = MERGE_TS
    tiles_per_seq = SEQ // ts
    t = pl.program_id(0)
    seq_start = (jnp.minimum(t, n_tiles - 1) % tiles_per_seq) == 0

    @pl.when(t == 0)
    def _():
        mg_ref[...] = jnp.zeros_like(mg_ref)

    mgp_ref[...] = mg_ref[...]
    o_ref[...] = x_ref[...] + jnp.dot(mgp_ref[...], wo_ref[...], preferred_element_type=F32)

    u = cc_ref[...].astype(F32) * cv_ref[...].astype(F32)
    uh = hc_ref[...].astype(F32) * hv_ref[...].astype(F32)
    u_ref[0:HALO, :] = jnp.where(seq_start, 0.0, uh)
    u_ref[HALO:HALO + ts, :] = u
    y = (cw_ref[0:1, :] * u
         + cw_ref[1:2, :] * u_ref[HALO - 1:HALO - 1 + ts, :]
         + cw_ref[2:3, :] * u_ref[HALO - 2:HALO - 2 + ts, :])
    c = (cb_ref[...].astype(F32) * y * _silu(zc_ref[...].astype(F32))).astype(BF16)

    gq = mqn_ref[...] * (MEM_HEAD_DIM ** -0.5)
    mo_parts = []
    for h in range(MEM_HEADS):
        cols = slice(h * MEM_HEAD_DIM, (h + 1) * MEM_HEAD_DIM)
        qh = _rms(mq_ref[:, cols].astype(F32), gq).astype(BF16)
        s = lax.dot_general(qh, mk_ref[:, cols], (((1,), (1,)), ((), ())),
                            preferred_element_type=F32)
        p = jnp.exp(s - jnp.max(s, axis=-1, keepdims=True))
        den = jnp.sum(p, axis=-1, keepdims=True)
        oh = jnp.dot(p.astype(BF16), mv_ref[:, cols], preferred_element_type=F32) / den
        mo_parts.append((oh * _silu(zm_ref[:, cols].astype(F32))).astype(BF16))
    mo = jnp.concatenate(mo_parts, axis=-1)

    ya = jnp.dot(a_ref[...], wa_ref[...], preferred_element_type=F32)
    yc = jnp.dot(c, wc_ref[...], preferred_element_type=F32)
    ym = jnp.dot(mo, wm_ref[...], preferred_element_type=F32)

    half = D_MODEL // 2
    for part, (ga, gc, gm) in enumerate(((g0a_ref, g1a_ref, g2a_ref), (g0b_ref, g1b_ref, g2b_ref))):
        cols = slice(part * half, (part + 1) * half)
        mg_ref[:, cols] = (jax.nn.sigmoid(ga[...].astype(F32)) * ya[:, cols]
                           + jax.nn.sigmoid(gc[...].astype(F32)) * yc[:, cols]
                           + jax.nn.sigmoid(gm[...].astype(F32)) * ym[:, cols]).astype(BF16)


def _merge(a, rest, x2, mk, mv, conv_w, mqn, wa, wc, wm, wo):
    n = x2.shape[0]
    ts = MERGE_TS
    n_tiles = n // ts
    tiles_per_seq = SEQ // ts
    half = D_MODEL // 2

    def cur(t):
        return jnp.minimum(t, n_tiles - 1)

    def prev(t):
        return jnp.maximum(t - 1, 0)

    def col_spec(col, width):
        return pl.BlockSpec((ts, width), lambda t: (cur(t), (col - REST_COL0) // width))

    def halo_spec(col):
        return pl.BlockSpec((HALO, CONV_WIDTH),
                            lambda t: (jnp.maximum(cur(t) * (ts // HALO) - 1, 0),
                                       (col - REST_COL0) // CONV_WIDTH))

    def const_spec(shape):
        return pl.BlockSpec(shape, lambda t: (0,) * len(shape), pipeline_mode=pl.Buffered(1))

    mem_spec = pl.BlockSpec((None, MEM_LEN, MEM_W), lambda t: (cur(t) // tiles_per_seq, 0, 0))
    in_specs = [
        pl.BlockSpec((ts, ATTN_OUT), lambda t: (cur(t), 0)),
        col_spec(COL_CB, CONV_WIDTH), col_spec(COL_CC, CONV_WIDTH),
        col_spec(COL_CV, CONV_WIDTH), col_spec(COL_ZC, CONV_WIDTH),
        halo_spec(COL_CC), halo_spec(COL_CV),
        col_spec(COL_MQ, MEM_W), col_spec(COL_ZM, MEM_W),
    ] + [col_spec(COL_G + k * half, half) for k in range(2 * N_BRANCH)] + [
        pl.BlockSpec((ts, D_MODEL), lambda t: (prev(t), 0)),
        mem_spec, mem_spec,
        const_spec((CONV_K, CONV_WIDTH)), const_spec((1, MEM_HEAD_DIM)),
        const_spec((ATTN_OUT, D_MODEL)), const_spec((CONV_WIDTH, D_MODEL)),
        const_spec((MEM_W, D_MODEL)), const_spec((D_MODEL, D_MODEL)),
    ]
    n_rest = 8 + 2 * N_BRANCH
    return pl.pallas_call(
        functools.partial(_merge_kernel, n_tiles=n_tiles),
        grid=(n_tiles + 1,),
        in_specs=in_specs,
        out_specs=pl.BlockSpec((ts, D_MODEL), lambda t: (prev(t), 0)),
        out_shape=jax.ShapeDtypeStruct((n, D_MODEL), F32),
        scratch_shapes=[pltpu.VMEM((HALO + ts, CONV_WIDTH), F32),
                        pltpu.VMEM((ts, D_MODEL), BF16), pltpu.VMEM((ts, D_MODEL), BF16)],
        compiler_params=pltpu.CompilerParams(
            dimension_semantics=("arbitrary",),
            vmem_limit_bytes=VMEM_LIMIT_BYTES),
        name="merge",
    )(a, *([rest] * n_rest), x2, mk, mv, conv_w, mqn, wa, wc, wm, wo)


def kernel(x, mem, norm_g, mem_norm_g, w_in, attn_q_norm, attn_k_norm, conv_w, mem_w_kv,
           mem_q_norm, mem_k_norm, w_br_attn, w_br_conv, w_br_mem, w_out):
    b, s, d = x.shape
    assert (s, d) == (SEQ, D_MODEL) and w_in.shape == (D_MODEL, IN_COLS)
    x2 = x.reshape(b * s, d)
    scale = HEAD_DIM ** -0.5
    head_gains = jnp.concatenate([jnp.repeat(attn_q_norm * scale, HEADS_PER_GROUP, axis=0),
                                  jnp.repeat(attn_k_norm, HEADS_PER_GROUP, axis=0),
                                  jnp.ones(((N_HEAD_TILES - N_NORM_TILES) * HEADS_PER_TILE, HEAD_DIM), F32)],
                                 axis=0)
    qkv, rest = _in_proj(x2, norm_g.reshape(1, d), head_gains, w_in.astype(BF16))
    mk, mv = _mem_kv(mem, mem_norm_g.reshape(1, d), mem_w_kv.astype(BF16),
                     mem_k_norm.reshape(1, MEM_HEAD_DIM))
    a = _attn(qkv)
    out = _merge(a, rest, x2, mk, mv, conv_w, mem_q_norm.reshape(1, MEM_HEAD_DIM),
                 w_br_attn.astype(BF16), w_br_conv.astype(BF16), w_br_mem.astype(BF16),
                 w_out.astype(BF16))
    return out.reshape(b, s, d)
```

```python
import functools

import jax
import jax.numpy as jnp
from jax import lax
from jax.experimental import pallas as pl
from jax.experimental.pallas import tpu as pltpu

D_MODEL = 2048
SEQ = 2048
HEAD_DIM = 128
ATTN_PATTERNS = ((128, 1), (512, 4), (2048, 16))
N_GROUPS = len(ATTN_PATTERNS)
HEADS_PER_GROUP = 4
GROUP_W = HEADS_PER_GROUP * HEAD_DIM
ATTN_QKV = N_GROUPS * GROUP_W
ATTN_OUT = GROUP_W
BLK = 128
CONV_WIDTH = 1024
CONV_K = 3
MEM_LEN = 256
MEM_HEADS = 4
MEM_HEAD_DIM = 256
MEM_W = MEM_HEADS * MEM_HEAD_DIM
N_BRANCH = 3
EPS = 1e-6
IN_COLS = 3 * ATTN_QKV + ATTN_OUT + 4 * CONV_WIDTH + 2 * MEM_W + N_BRANCH * D_MODEL

COL_Q = 0
COL_K = ATTN_QKV
COL_V = 2 * ATTN_QKV
COL_ZA = 3 * ATTN_QKV
COL_CB = COL_ZA + ATTN_OUT
COL_CC = COL_CB + CONV_WIDTH
COL_CV = COL_CC + CONV_WIDTH
COL_ZC = COL_CV + CONV_WIDTH
COL_MQ = COL_ZC + CONV_WIDTH
COL_ZM = COL_MQ + MEM_W
COL_G = COL_ZM + MEM_W

F32 = jnp.float32
BF16 = jnp.bfloat16

VMEM_LIMIT_BYTES = 56 * 1024 * 1024

PROJ_TM = 1024
PROJ_TN = 1024
MERGE_TS = 256
HALO = 16
ROW_CHUNK = 256


def _rms(t, gain):
    return t * lax.rsqrt(jnp.mean(t * t, axis=-1, keepdims=True) + EPS) * gain


def _silu(z):
    return z * jax.nn.sigmoid(z)


CLS = 4
CLS_LEN = SEQ // CLS
HEADS_PER_TILE = PROJ_TN // HEAD_DIM
N_QKV_HEADS = 3 * N_GROUPS * HEADS_PER_GROUP
N_NORM_HEADS = 2 * N_GROUPS * HEADS_PER_GROUP
N_HEAD_TILES = (COL_ZA + ATTN_OUT) // PROJ_TN
N_NORM_TILES = N_NORM_HEADS // HEADS_PER_TILE
REST_COL0 = N_HEAD_TILES * PROJ_TN
REST_COLS = IN_COLS - REST_COL0
assert N_NORM_HEADS % HEADS_PER_TILE == 0 and SEQ % PROJ_TM == 0 and PROJ_TM % CLS == 0
assert COL_ZA == N_QKV_HEADS * HEAD_DIM and (COL_ZA + ATTN_OUT) % PROJ_TN == 0
SILU_TILES = (COL_ZC // PROJ_TN, COL_ZM // PROJ_TN)
MQ_TILE = COL_MQ // PROJ_TN
assert CONV_WIDTH == PROJ_TN and MEM_W == PROJ_TN and COL_ZC % PROJ_TN == 0 and COL_MQ % PROJ_TN == 0


def _in_proj_kernel(x_ref, g_ref, hg_ref, mg_ref, w_ref, qkv_ref, o_ref, h_ref, hp_ref, tmp_ref):
    j = pl.program_id(1)
    rows_per_cls = PROJ_TM // CLS

    @pl.when(j == 0)
    def _():
        def body(c, _):
            rows = pl.ds(pl.multiple_of(c * ROW_CHUNK, ROW_CHUNK), ROW_CHUNK)
            hf = _rms(x_ref[rows, :], g_ref[...])
            h_ref[rows, :] = hf.astype(BF16)
            piece = ROW_CHUNK // CLS
            for s in range(D_MODEL // HEAD_DIM):
                lanes = slice(s * HEAD_DIM, (s + 1) * HEAD_DIM)
                tmp_ref[s] = hf[:, lanes]
                for b in range(CLS):
                    dst = pl.ds(pl.multiple_of(b * rows_per_cls + c * piece, piece), piece)
                    hp_ref[dst, lanes] = tmp_ref[s, pl.ds(b, piece, stride=CLS), :].astype(BF16)
            return _
        lax.fori_loop(0, PROJ_TM // ROW_CHUNK, body, None)

    def emit_heads(normed):
        res = jnp.dot(hp_ref[...], w_ref[...], preferred_element_type=F32)
        gains = hg_ref[pl.ds(pl.multiple_of(j * HEADS_PER_TILE, HEADS_PER_TILE), HEADS_PER_TILE), :]
        for s in range(HEADS_PER_TILE):
            r = res[:, s * HEAD_DIM:(s + 1) * HEAD_DIM]
            if normed:
                r = _rms(r, gains[s:s + 1, :])
            for b in range(CLS):
                qkv_ref[s, b] = r[b * rows_per_cls:(b + 1) * rows_per_cls]

    @pl.when(j < N_NORM_TILES)
    def _():
        emit_heads(True)

    @pl.when((j >= N_NORM_TILES) & (j < N_HEAD_TILES))
    def _():
        emit_heads(False)

    def project():
        return jnp.dot(h_ref[...], w_ref[...], preferred_element_type=F32)

    is_silu = (j == SILU_TILES[0]) | (j == SILU_TILES[1])

    @pl.when(is_silu)
    def _():
        o_ref[...] = _silu(project()).astype(o_ref.dtype)

    @pl.when(j == MQ_TILE)
    def _():
        res = project()
        for h in range(MEM_HEADS):
            cols = slice(h * MEM_HEAD_DIM, (h + 1) * MEM_HEAD_DIM)
            o_ref[:, cols] = _rms(res[:, cols], mg_ref[...]).astype(o_ref.dtype)

    @pl.when((j >= N_HEAD_TILES) & jnp.logical_not(is_silu) & (j != MQ_TILE))
    def _():
        o_ref[...] = project().astype(o_ref.dtype)


def _in_proj(x2, g, head_gains, mem_q_gain, w):
    n = x2.shape[0]
    tiles_per_seq = SEQ // PROJ_TM
    rows = PROJ_TM // CLS
    return pl.pallas_call(
        _in_proj_kernel,
        grid=(n // PROJ_TM, IN_COLS // PROJ_TN),
        in_specs=[
            pl.BlockSpec((PROJ_TM, D_MODEL), lambda i, j: (i, 0)),
            pl.BlockSpec((1, D_MODEL), lambda i, j: (0, 0)),
            pl.BlockSpec((N_HEAD_TILES * HEADS_PER_TILE, HEAD_DIM), lambda i, j: (0, 0)),
            pl.BlockSpec((1, MEM_HEAD_DIM), lambda i, j: (0, 0)),
            pl.BlockSpec((D_MODEL, PROJ_TN), lambda i, j: (0, j)),
        ],
        out_specs=[
            pl.BlockSpec((HEADS_PER_TILE, None, CLS, rows, HEAD_DIM),
                         lambda i, j: (jnp.minimum(j, N_HEAD_TILES - 1), i // tiles_per_seq, 0,
                                       i % tiles_per_seq, 0)),
            pl.BlockSpec((PROJ_TM, PROJ_TN),
                         lambda i, j: (i, jnp.maximum(j, N_HEAD_TILES) - N_HEAD_TILES)),
        ],
        out_shape=[
            jax.ShapeDtypeStruct((N_HEAD_TILES * HEADS_PER_TILE, n // SEQ, CLS, CLS_LEN, HEAD_DIM), F32),
            jax.ShapeDtypeStruct((n, REST_COLS), BF16),
        ],
        scratch_shapes=[pltpu.VMEM((PROJ_TM, D_MODEL), BF16), pltpu.VMEM((PROJ_TM, D_MODEL), BF16),
                        pltpu.VMEM((D_MODEL // HEAD_DIM, ROW_CHUNK, HEAD_DIM), F32)],
        compiler_params=pltpu.CompilerParams(
            dimension_semantics=("arbitrary", "arbitrary"),
            vmem_limit_bytes=VMEM_LIMIT_BYTES),
        name="in_proj",
    )(x2, g, head_gains, mem_q_gain, w)


def _mem_kv_kernel(mem_ref, g_ref, w_ref, kn_ref, mk_ref, mv_ref):
    mh = _rms(mem_ref[...], g_ref[...]).astype(BF16)
    kv = jnp.dot(mh, w_ref[...], preferred_element_type=F32)
    for h in range(MEM_HEADS):
        cols = slice(h * MEM_HEAD_DIM, (h + 1) * MEM_HEAD_DIM)
        mk_ref[:, cols] = _rms(kv[:, cols], kn_ref[...]).astype(BF16)
    mv_ref[...] = kv[:, MEM_W:].astype(BF16)


def _mem_kv(mem, g, w, kn):
    b = mem.shape[0]
    return pl.pallas_call(
        _mem_kv_kernel,
        grid=(b,),
        in_specs=[
            pl.BlockSpec((None, MEM_LEN, D_MODEL), lambda i: (i, 0, 0)),
            pl.BlockSpec((1, D_MODEL), lambda i: (0, 0)),
            pl.BlockSpec((D_MODEL, 2 * MEM_W), lambda i: (0, 0)),
            pl.BlockSpec((1, MEM_HEAD_DIM), lambda i: (0, 0)),
        ],
        out_specs=[
            pl.BlockSpec((None, MEM_LEN, MEM_W), lambda i: (i, 0, 0)),
            pl.BlockSpec((None, MEM_LEN, MEM_W), lambda i: (i, 0, 0)),
        ],
        out_shape=[jax.ShapeDtypeStruct((b, MEM_LEN, MEM_W), BF16)] * 2,
        compiler_params=pltpu.CompilerParams(
            dimension_semantics=("arbitrary",),
            vmem_limit_bytes=VMEM_LIMIT_BYTES),
        name="mem_kv",
    )(mem, g, w, kn)


def _band_biases(bias_ref):
    r = lax.broadcasted_iota(jnp.int32, (BLK, 2 * BLK), 0)
    c = lax.broadcasted_iota(jnp.int32, (BLK, 2 * BLK), 1)

    def put(idx, diff):
        in_band = lax.bitcast_convert_type(diff, jnp.uint32) <= jnp.uint32(BLK)
        bias_ref[idx] = jnp.where(in_band, 0.0, -jnp.inf)

    put(0, c - r)
    piece = BLK // CLS
    b, jq = r // piece, r % piece
    bk, jk = c // (2 * piece), c % (2 * piece)
    put(1, CLS * (piece + jq - jk) + (b - bk))
    bk, jk = c // piece, c % piece
    put(2, CLS * (jq - jk) + (b - bk))


def _load_rows(ref, h, pieces):
    parts = [ref[h, b, rows, :] for b, rows in pieces]
    return parts[0] if len(parts) == 1 else jnp.concatenate(parts, axis=0)


def _store_rows(ref, h, pieces, val):
    off = 0
    for b, rows in pieces:
        ref[h, b, rows, :] = val[off:off + rows.size]
        off += rows.size


def _attn_block(qp, kp, bias, first, q_ref, k_ref, v_ref, acc_ref, m_ref, l_ref):
    nk = bias.shape[1]
    ones = jnp.ones((nk, HEAD_DIM), BF16)
    heads_per_dot = 2 * BLK // nk
    zeros = jnp.zeros((nk, HEAD_DIM), BF16)
    for h0 in range(0, HEADS_PER_GROUP, heads_per_dot):
        hs = range(h0, h0 + heads_per_dot)
        qb = jnp.concatenate([_load_rows(q_ref, h, qp) for h in hs], axis=-1).astype(BF16)
        ks = [_load_rows(k_ref, h, kp).astype(BF16) for h in hs]
        vs = [_load_rows(v_ref, h, kp).astype(BF16) for h in hs]
        if heads_per_dot == 1:
            kb = ks[0]
            vb = jnp.concatenate([vs[0], ones], axis=-1)
            bias_w = bias
        else:
            kb = jnp.concatenate([jnp.concatenate([ks[0], zeros], axis=-1),
                                  jnp.concatenate([zeros, ks[1]], axis=-1)], axis=0)
            vb = jnp.concatenate([jnp.concatenate([vs[0], ones, zeros, zeros], axis=-1),
                                  jnp.concatenate([zeros, zeros, vs[1], ones], axis=-1)], axis=0)
            bias_w = jnp.concatenate([bias, bias], axis=-1)
        s = lax.dot_general(qb, kb, (((1,), (1,)), ((), ())), preferred_element_type=F32) + bias_w
        w = s.shape[1] // heads_per_dot
        m_olds, m_news = [], []
        for i, h in enumerate(hs):
            mb = jnp.max(s[:, i * w:(i + 1) * w], axis=-1, keepdims=True)
            if first:
                m_olds.append(None)
                m_news.append(jnp.broadcast_to(mb, (BLK, HEAD_DIM)))
            else:
                m_olds.append(_load_rows(m_ref, h, qp))
                m_news.append(jnp.maximum(m_olds[-1], mb))
        m_wide = jnp.concatenate([m for m in m_news for _ in range(w // HEAD_DIM)], axis=-1)
        p = jnp.exp(s - m_wide).astype(BF16)
        pv = jnp.dot(p, vb, preferred_element_type=F32)
        for i, h in enumerate(hs):
            acc_new = pv[:, 2 * i * HEAD_DIM:(2 * i + 1) * HEAD_DIM]
            l_new = pv[:, (2 * i + 1) * HEAD_DIM:(2 * i + 2) * HEAD_DIM]
            if not first:
                alpha = jnp.exp(m_olds[i] - m_news[i])
                acc_new = alpha * _load_rows(acc_ref, h, qp) + acc_new
                l_new = alpha * _load_rows(l_ref, h, qp) + l_new
            _store_rows(m_ref, h, qp, m_news[i])
            _store_rows(l_ref, h, qp, l_new)
            _store_rows(acc_ref, h, qp, acc_new)


def _attn_group(d, first, q_ref, k_ref, v_ref, bias_ref, acc_ref, m_ref, l_ref):
    block = functools.partial(_attn_block, first=first, q_ref=q_ref, k_ref=k_ref, v_ref=v_ref,
                              acc_ref=acc_ref, m_ref=m_ref, l_ref=l_ref)
    if d == 16:
        for r in range(d):
            a, b = divmod(r, CLS)
            rows = [(b, pl.ds(a, BLK, stride=CLS))]
            block(rows, rows, bias_ref[0, :, BLK:])
    elif d == 4:
        for b in range(CLS):
            for bi in range(CLS_LEN // BLK):
                q0 = bi * BLK
                if bi == 0:
                    block([(b, pl.ds(q0, BLK))], [(b, pl.ds(q0, BLK))], bias_ref[0, :, BLK:])
                else:
                    block([(b, pl.ds(q0, BLK))], [(b, pl.ds(q0 - BLK, 2 * BLK))], bias_ref[0])
    else:
        piece = BLK // CLS
        for bi in range(SEQ // BLK):
            qp = [(b, pl.ds(bi * piece, piece)) for b in range(CLS)]
            if bi == 0:
                block(qp, qp, bias_ref[2, :, :BLK])
            else:
                kp = [(b, pl.ds((bi - 1) * piece, 2 * piece)) for b in range(CLS)]
                block(qp, kp, bias_ref[1])


def _attn_kernel(q_ref, k_ref, v_ref, z_ref, o_ref, bias_ref, acc_ref, m_ref, l_ref, nat_ref):
    step = pl.program_id(1)
    _band_biases(bias_ref)

    for s in range(N_GROUPS):
        @pl.when(step == s)
        def _(s=s):
            _attn_group(ATTN_PATTERNS[N_GROUPS - 1 - s][1], s == 0, q_ref, k_ref, v_ref,
                        bias_ref, acc_ref, m_ref, l_ref)

    @pl.when(step == N_GROUPS - 1)
    def _():
        half = CLS_LEN // 2
        for h in range(HEADS_PER_GROUP):
            for b in range(CLS):
                for k in range(2):
                    src = pl.ds(k * half, half)
                    nat_ref[h, pl.ds(b + k * half * CLS, half, stride=CLS), :] = (
                        acc_ref[h, b, src, :] / l_ref[h, b, src, :]
                        * _silu(z_ref[h, b, src, :]))

        def fin(c, _):
            rows = pl.ds(pl.multiple_of(c * ROW_CHUNK, ROW_CHUNK), ROW_CHUNK)
            for h in range(HEADS_PER_GROUP):
                o_ref[rows, h * HEAD_DIM:(h + 1) * HEAD_DIM] = nat_ref[h, rows, :].astype(o_ref.dtype)
            return _
        lax.fori_loop(0, SEQ // ROW_CHUNK, fin, None)


def _attn(qkv):
    nb = qkv.shape[1]
    n = nb * SEQ

    def head_spec(kind):
        return pl.BlockSpec((HEADS_PER_GROUP, None, CLS, CLS_LEN, HEAD_DIM),
                            lambda i, s: (kind * N_GROUPS + N_GROUPS - 1 - s, i, 0, 0, 0))

    slab = pltpu.VMEM((HEADS_PER_GROUP, CLS, CLS_LEN, HEAD_DIM), F32)
    return pl.pallas_call(
        _attn_kernel,
        grid=(nb, N_GROUPS),
        in_specs=[
            head_spec(0), head_spec(1), head_spec(2),
            pl.BlockSpec((HEADS_PER_GROUP, None, CLS, CLS_LEN, HEAD_DIM),
                         lambda i, s: (N_QKV_HEADS // HEADS_PER_GROUP, i, 0, 0, 0)),
        ],
        out_specs=pl.BlockSpec((SEQ, ATTN_OUT), lambda i, s: (i, 0)),
        out_shape=jax.ShapeDtypeStruct((n, ATTN_OUT), BF16),
        scratch_shapes=[pltpu.VMEM((3, BLK, 2 * BLK), F32), slab, slab, slab,
                        pltpu.VMEM((HEADS_PER_GROUP, SEQ, HEAD_DIM), F32)],
        compiler_params=pltpu.CompilerParams(
            dimension_semantics=("arbitrary", "arbitrary"),
            vmem_limit_bytes=VMEM_LIMIT_BYTES),
        name="attn",
    )(qkv, qkv, qkv, qkv)


def _merge_kernel(a_ref, cb_ref, cc_ref, cv_ref, zc_ref, hc_ref, hv_ref, mq_ref, zm_ref,
                  g0a_ref, g0b_ref, g1a_ref, g1b_ref, g2a_ref, g2b_ref,
                  x_ref, mk_ref, mv_ref, cw_ref,
                  wa_ref, wc_ref, wm_ref, wo_ref, o_ref, u_ref, mg_ref, mgp_ref, *, n_tiles):
    ts = MERGE_TS
    tiles_per_seq = SEQ // ts
    t = pl.program_id(0)
    seq_start = (jnp.minimum(t, n_tiles - 1) % tiles_per_seq) == 0

    @pl.when(t == 0)
    def _():
        mg_ref[...] = jnp.zeros_like(mg_ref)

    mgp_ref[...] = mg_ref[...]
    o_ref[...] = x_ref[...] + jnp.dot(mgp_ref[...], wo_ref[...], preferred_element_type=F32)

    u = cc_ref[...].astype(F32) * cv_ref[...].astype(F32)
    uh = hc_ref[...].astype(F32) * hv_ref[...].astype(F32)
    u_ref[0:HALO, :] = jnp.where(seq_start, 0.0, uh)
    u_ref[HALO:HALO + ts, :] = u
    y = (cw_ref[0:1, :] * u
         + cw_ref[1:2, :] * u_ref[HALO - 1:HALO - 1 + ts, :]
         + cw_ref[2:3, :] * u_ref[HALO - 2:HALO - 2 + ts, :])
    c = (cb_ref[...].astype(F32) * y * zc_ref[...].astype(F32)).astype(BF16)

    mo_parts = []
    for h in range(MEM_HEADS):
        cols = slice(h * MEM_HEAD_DIM, (h + 1) * MEM_HEAD_DIM)
        s = lax.dot_general(mq_ref[:, cols], mk_ref[:, cols], (((1,), (1,)), ((), ())),
                            preferred_element_type=F32)
        p = jnp.exp(s - jnp.max(s, axis=-1, keepdims=True))
        den = jnp.sum(p, axis=-1, keepdims=True)
        oh = jnp.dot(p.astype(BF16), mv_ref[:, cols], preferred_element_type=F32) / den
        mo_parts.append((oh * zm_ref[:, cols].astype(F32)).astype(BF16))
    mo = jnp.concatenate(mo_parts, axis=-1)

    ya = jnp.dot(a_ref[...], wa_ref[...], preferred_element_type=F32)
    yc = jnp.dot(c, wc_ref[...], preferred_element_type=F32)
    ym = jnp.dot(mo, wm_ref[...], preferred_element_type=F32)

    half = D_MODEL // 2
    for part, (ga, gc, gm) in enumerate(((g0a_ref, g1a_ref, g2a_ref), (g0b_ref, g1b_ref, g2b_ref))):
        cols = slice(part * half, (part + 1) * half)
        mg_ref[:, cols] = (jax.nn.sigmoid(ga[...].astype(F32)) * ya[:, cols]
                           + jax.nn.sigmoid(gc[...].astype(F32)) * yc[:, cols]
                           + jax.nn.sigmoid(gm[...].astype(F32)) * ym[:, cols]).astype(BF16)


def _merge(a, rest, x2, mk, mv, conv_w, wa, wc, wm, wo):
    n = x2.shape[0]
    ts = MERGE_TS
    n_tiles = n // ts
    tiles_per_seq = SEQ // ts
    half = D_MODEL // 2

    def cur(t):
        return jnp.minimum(t, n_tiles - 1)

    def prev(t):
        return jnp.maximum(t - 1, 0)

    def col_spec(col, width):
        return pl.BlockSpec((ts, width), lambda t: (cur(t), (col - REST_COL0) // width))

    def halo_spec(col):
        return pl.BlockSpec((HALO, CONV_WIDTH),
                            lambda t: (jnp.maximum(cur(t) * (ts // HALO) - 1, 0),
                                       (col - REST_COL0) // CONV_WIDTH))

    def const_spec(shape):
        return pl.BlockSpec(shape, lambda t: (0,) * len(shape), pipeline_mode=pl.Buffered(1))

    mem_spec = pl.BlockSpec((None, MEM_LEN, MEM_W), lambda t: (cur(t) // tiles_per_seq, 0, 0))
    in_specs = [
        pl.BlockSpec((ts, ATTN_OUT), lambda t: (cur(t), 0)),
        col_spec(COL_CB, CONV_WIDTH), col_spec(COL_CC, CONV_WIDTH),
        col_spec(COL_CV, CONV_WIDTH), col_spec(COL_ZC, CONV_WIDTH),
        halo_spec(COL_CC), halo_spec(COL_CV),
        col_spec(COL_MQ, MEM_W), col_spec(COL_ZM, MEM_W),
    ] + [col_spec(COL_G + k * half, half) for k in range(2 * N_BRANCH)] + [
        pl.BlockSpec((ts, D_MODEL), lambda t: (prev(t), 0)),
        mem_spec, mem_spec,
        const_spec((CONV_K, CONV_WIDTH)),
        const_spec((ATTN_OUT, D_MODEL)), const_spec((CONV_WIDTH, D_MODEL)),
        const_spec((MEM_W, D_MODEL)), const_spec((D_MODEL, D_MODEL)),
    ]
    n_rest = 8 + 2 * N_BRANCH
    return pl.pallas_call(
        functools.partial(_merge_kernel, n_tiles=n_tiles),
        grid=(n_tiles + 1,),
        in_specs=in_specs,
        out_specs=pl.BlockSpec((ts, D_MODEL), lambda t: (prev(t), 0)),
        out_shape=jax.ShapeDtypeStruct((n, D_MODEL), F32),
        scratch_shapes=[pltpu.VMEM((HALO + ts, CONV_WIDTH), F32),
                        pltpu.VMEM((ts, D_MODEL), BF16), pltpu.VMEM((ts, D_MODEL), BF16)],
        compiler_params=pltpu.CompilerParams(
            dimension_semantics=("arbitrary",),
            vmem_limit_bytes=VMEM_LIMIT_BYTES),
        name="merge",
    )(a, *([rest] * n_rest), x2, mk, mv, conv_w, wa, wc, wm, wo)


def kernel(x, mem, norm_g, mem_norm_g, w_in, attn_q_norm, attn_k_norm, conv_w, mem_w_kv,
           mem_q_norm, mem_k_norm, w_br_attn, w_br_conv, w_br_mem, w_out):
    b, s, d = x.shape
    assert (s, d) == (SEQ, D_MODEL) and w_in.shape == (D_MODEL, IN_COLS)
    x2 = x.reshape(b * s, d)
    scale = HEAD_DIM ** -0.5
    head_gains = jnp.concatenate([jnp.repeat(attn_q_norm * scale, HEADS_PER_GROUP, axis=0),
                                  jnp.repeat(attn_k_norm, HEADS_PER_GROUP, axis=0),
                                  jnp.ones(((N_HEAD_TILES - N_NORM_TILES) * HEADS_PER_TILE, HEAD_DIM), F32)],
                                 axis=0)
    mem_q_gain = mem_q_norm.reshape(1, MEM_HEAD_DIM) * (MEM_HEAD_DIM ** -0.5)
    qkv, rest = _in_proj(x2, norm_g.reshape(1, d), head_gains, mem_q_gain, w_in.astype(BF16))
    mk, mv = _mem_kv(mem, mem_norm_g.reshape(1, d), mem_w_kv.astype(BF16),
                     mem_k_norm.reshape(1, MEM_HEAD_DIM))
    a = _attn(qkv)
    out = _merge(a, rest, x2, mk, mv, conv_w,
                 w_br_attn.astype(BF16), w_br_conv.astype(BF16), w_br_mem.astype(BF16),
                 w_out.astype(BF16))
    return out.reshape(b, s, d)
```

```python
import functools

import jax
import jax.numpy as jnp
from jax import lax
from jax.experimental import pallas as pl
from jax.experimental.pallas import tpu as pltpu

D_MODEL = 2048
SEQ = 2048
HEAD_DIM = 128
ATTN_PATTERNS = ((128, 1), (512, 4), (2048, 16))
N_GROUPS = len(ATTN_PATTERNS)
HEADS_PER_GROUP = 4
GROUP_W = HEADS_PER_GROUP * HEAD_DIM
ATTN_QKV = N_GROUPS * GROUP_W
ATTN_OUT = GROUP_W
BLK = 128
CONV_WIDTH = 1024
CONV_K = 3
MEM_LEN = 256
MEM_HEADS = 4
MEM_HEAD_DIM = 256
MEM_W = MEM_HEADS * MEM_HEAD_DIM
N_BRANCH = 3
EPS = 1e-6
IN_COLS = 3 * ATTN_QKV + ATTN_OUT + 4 * CONV_WIDTH + 2 * MEM_W + N_BRANCH * D_MODEL

COL_Q = 0
COL_K = ATTN_QKV
COL_V = 2 * ATTN_QKV
COL_ZA = 3 * ATTN_QKV
COL_CB = COL_ZA + ATTN_OUT
COL_CC = COL_CB + CONV_WIDTH
COL_CV = COL_CC + CONV_WIDTH
COL_ZC = COL_CV + CONV_WIDTH
COL_MQ = COL_ZC + CONV_WIDTH
COL_ZM = COL_MQ + MEM_W
COL_G = COL_ZM + MEM_W

F32 = jnp.float32
BF16 = jnp.bfloat16

VMEM_LIMIT_BYTES = 56 * 1024 * 1024

PROJ_TM = 1024
PROJ_TN = 1024
MERGE_TS = 256
HALO = 16
ROW_CHUNK = 256


def _rms(t, gain):
    return t * lax.rsqrt(jnp.mean(t * t, axis=-1, keepdims=True) + EPS) * gain


def _silu(z):
    return z * jax.nn.sigmoid(z)


CLS = 4
CLS_LEN = SEQ // CLS
HEADS_PER_TILE = PROJ_TN // HEAD_DIM
N_QKV_HEADS = 3 * N_GROUPS * HEADS_PER_GROUP
N_NORM_HEADS = 2 * N_GROUPS * HEADS_PER_GROUP
N_HEAD_TILES = (COL_ZA + ATTN_OUT) // PROJ_TN
N_NORM_TILES = N_NORM_HEADS // HEADS_PER_TILE
REST_COL0 = N_HEAD_TILES * PROJ_TN
REST_COLS = IN_COLS - REST_COL0
assert N_NORM_HEADS % HEADS_PER_TILE == 0 and SEQ % PROJ_TM == 0 and PROJ_TM % CLS == 0
assert COL_ZA == N_QKV_HEADS * HEAD_DIM and (COL_ZA + ATTN_OUT) % PROJ_TN == 0


def _in_proj_kernel(x_ref, g_ref, hg_ref, w_ref, qkv_ref, o_ref, h_ref, hp_ref, tmp_ref):
    j = pl.program_id(1)
    rows_per_cls = PROJ_TM // CLS

    @pl.when(j == 0)
    def _():
        def body(c, _):
            rows = pl.ds(pl.multiple_of(c * ROW_CHUNK, ROW_CHUNK), ROW_CHUNK)
            hf = _rms(x_ref[rows, :], g_ref[...])
            h_ref[rows, :] = hf.astype(BF16)
            piece = ROW_CHUNK // CLS
            for s in range(D_MODEL // HEAD_DIM):
                lanes = slice(s * HEAD_DIM, (s + 1) * HEAD_DIM)
                tmp_ref[s] = hf[:, lanes]
                for b in range(CLS):
                    dst = pl.ds(pl.multiple_of(b * rows_per_cls + c * piece, piece), piece)
                    hp_ref[dst, lanes] = tmp_ref[s, pl.ds(b, piece, stride=CLS), :].astype(BF16)
            return _
        lax.fori_loop(0, PROJ_TM // ROW_CHUNK, body, None)

    def emit_heads(normed):
        res = jnp.dot(hp_ref[...], w_ref[...], preferred_element_type=F32)
        gains = hg_ref[pl.ds(pl.multiple_of(j * HEADS_PER_TILE, HEADS_PER_TILE), HEADS_PER_TILE), :]
        for s in range(HEADS_PER_TILE):
            r = res[:, s * HEAD_DIM:(s + 1) * HEAD_DIM]
            if normed:
                r = _rms(r, gains[s:s + 1, :])
            for b in range(CLS):
                qkv_ref[s, b] = r[b * rows_per_cls:(b + 1) * rows_per_cls].astype(qkv_ref.dtype)

    @pl.when(j < N_NORM_TILES)
    def _():
        emit_heads(True)

    @pl.when((j >= N_NORM_TILES) & (j < N_HEAD_TILES))
    def _():
        emit_heads(False)

    @pl.when(j >= N_HEAD_TILES)
    def _():
        o_ref[...] = jnp.dot(h_ref[...], w_ref[...], preferred_element_type=F32).astype(o_ref.dtype)


def _in_proj(x2, g, head_gains, w):
    n = x2.shape[0]
    tiles_per_seq = SEQ // PROJ_TM
    rows = PROJ_TM // CLS
    return pl.pallas_call(
        _in_proj_kernel,
        grid=(n // PROJ_TM, IN_COLS // PROJ_TN),
        in_specs=[
            pl.BlockSpec((PROJ_TM, D_MODEL), lambda i, j: (i, 0)),
            pl.BlockSpec((1, D_MODEL), lambda i, j: (0, 0)),
            pl.BlockSpec((N_HEAD_TILES * HEADS_PER_TILE, HEAD_DIM), lambda i, j: (0, 0)),
            pl.BlockSpec((D_MODEL, PROJ_TN), lambda i, j: (0, j)),
        ],
        out_specs=[
            pl.BlockSpec((HEADS_PER_TILE, None, CLS, rows, HEAD_DIM),
                         lambda i, j: (jnp.minimum(j, N_HEAD_TILES - 1), i // tiles_per_seq, 0,
                                       i % tiles_per_seq, 0)),
            pl.BlockSpec((PROJ_TM, PROJ_TN),
                         lambda i, j: (i, jnp.maximum(j, N_HEAD_TILES) - N_HEAD_TILES)),
        ],
        out_shape=[
            jax.ShapeDtypeStruct((N_HEAD_TILES * HEADS_PER_TILE, n // SEQ, CLS, CLS_LEN, HEAD_DIM), BF16),
            jax.ShapeDtypeStruct((n, REST_COLS), BF16),
        ],
        scratch_shapes=[pltpu.VMEM((PROJ_TM, D_MODEL), BF16), pltpu.VMEM((PROJ_TM, D_MODEL), BF16),
                        pltpu.VMEM((D_MODEL // HEAD_DIM, ROW_CHUNK, HEAD_DIM), F32)],
        compiler_params=pltpu.CompilerParams(
            dimension_semantics=("arbitrary", "arbitrary"),
            vmem_limit_bytes=VMEM_LIMIT_BYTES),
        name="in_proj",
    )(x2, g, head_gains, w)


def _mem_kv_kernel(mem_ref, g_ref, w_ref, kn_ref, mk_ref, mv_ref):
    mh = _rms(mem_ref[...], g_ref[...]).astype(BF16)
    kv = jnp.dot(mh, w_ref[...], preferred_element_type=F32)
    for h in range(MEM_HEADS):
        cols = slice(h * MEM_HEAD_DIM, (h + 1) * MEM_HEAD_DIM)
        mk_ref[:, cols] = _rms(kv[:, cols], kn_ref[...]).astype(BF16)
    mv_ref[...] = kv[:, MEM_W:].astype(BF16)


def _mem_kv(mem, g, w, kn):
    b = mem.shape[0]
    return pl.pallas_call(
        _mem_kv_kernel,
        grid=(b,),
        in_specs=[
            pl.BlockSpec((None, MEM_LEN, D_MODEL), lambda i: (i, 0, 0)),
            pl.BlockSpec((1, D_MODEL), lambda i: (0, 0)),
            pl.BlockSpec((D_MODEL, 2 * MEM_W), lambda i: (0, 0)),
            pl.BlockSpec((1, MEM_HEAD_DIM), lambda i: (0, 0)),
        ],
        out_specs=[
            pl.BlockSpec((None, MEM_LEN, MEM_W), lambda i: (i, 0, 0)),
            pl.BlockSpec((None, MEM_LEN, MEM_W), lambda i: (i, 0, 0)),
        ],
        out_shape=[jax.ShapeDtypeStruct((b, MEM_LEN, MEM_W), BF16)] * 2,
        compiler_params=pltpu.CompilerParams(
            dimension_semantics=("arbitrary",),
            vmem_limit_bytes=VMEM_LIMIT_BYTES),
        name="mem_kv",
    )(mem, g, w, kn)


def _band_biases(bias_ref):
    r = lax.broadcasted_iota(jnp.int32, (BLK, 2 * BLK), 0)
    c = lax.broadcasted_iota(jnp.int32, (BLK, 2 * BLK), 1)

    def put(idx, diff):
        in_band = lax.bitcast_convert_type(diff, jnp.uint32) <= jnp.uint32(BLK)
        bias_ref[idx] = jnp.where(in_band, 0.0, -jnp.inf)

    put(0, c - r)
    piece = BLK // CLS
    b, jq = r // piece, r % piece
    bk, jk = c // (2 * piece), c % (2 * piece)
    put(1, CLS * (piece + jq - jk) + (b - bk))
    bk, jk = c // piece, c % piece
    put(2, CLS * (jq - jk) + (b - bk))


def _load_rows(ref, h, pieces):
    parts = [ref[h, b, rows, :] for b, rows in pieces]
    return parts[0] if len(parts) == 1 else jnp.concatenate(parts, axis=0)


def _store_rows(ref, h, pieces, val):
    off = 0
    for b, rows in pieces:
        ref[h, b, rows, :] = val[off:off + rows.size]
        off += rows.size


def _attn_block(qp, kp, bias, first, q_ref, k_ref, v_ref, acc_ref, m_ref, l_ref):
    nk = bias.shape[1]
    ones = jnp.ones((nk, HEAD_DIM), BF16)
    heads_per_dot = 2 * BLK // nk
    zeros = jnp.zeros((nk, HEAD_DIM), BF16)
    for h0 in range(0, HEADS_PER_GROUP, heads_per_dot):
        hs = range(h0, h0 + heads_per_dot)
        qb = jnp.concatenate([_load_rows(q_ref, h, qp) for h in hs], axis=-1).astype(BF16)
        ks = [_load_rows(k_ref, h, kp).astype(BF16) for h in hs]
        vs = [_load_rows(v_ref, h, kp).astype(BF16) for h in hs]
        if heads_per_dot == 1:
            kb = ks[0]
            vb = jnp.concatenate([vs[0], ones], axis=-1)
            bias_w = bias
        else:
            kb = jnp.concatenate([jnp.concatenate([ks[0], zeros], axis=-1),
                                  jnp.concatenate([zeros, ks[1]], axis=-1)], axis=0)
            vb = jnp.concatenate([jnp.concatenate([vs[0], ones, zeros, zeros], axis=-1),
                                  jnp.concatenate([zeros, zeros, vs[1], ones], axis=-1)], axis=0)
            bias_w = jnp.concatenate([bias, bias], axis=-1)
        s = lax.dot_general(qb, kb, (((1,), (1,)), ((), ())), preferred_element_type=F32) + bias_w
        w = s.shape[1] // heads_per_dot
        m_olds, m_news = [], []
        for i, h in enumerate(hs):
            mb = jnp.max(s[:, i * w:(i + 1) * w], axis=-1, keepdims=True)
            if first:
                m_olds.append(None)
                m_news.append(jnp.broadcast_to(mb, (BLK, HEAD_DIM)))
            else:
                m_olds.append(_load_rows(m_ref, h, qp))
                m_news.append(jnp.maximum(m_olds[-1], mb))
        m_wide = jnp.concatenate([m for m in m_news for _ in range(w // HEAD_DIM)], axis=-1)
        p = jnp.exp(s - m_wide).astype(BF16)
        pv = jnp.dot(p, vb, preferred_element_type=F32)
        for i, h in enumerate(hs):
            acc_new = pv[:, 2 * i * HEAD_DIM:(2 * i + 1) * HEAD_DIM]
            l_new = pv[:, (2 * i + 1) * HEAD_DIM:(2 * i + 2) * HEAD_DIM]
            if not first:
                alpha = jnp.exp(m_olds[i] - m_news[i])
                acc_new = alpha * _load_rows(acc_ref, h, qp) + acc_new
                l_new = alpha * _load_rows(l_ref, h, qp) + l_new
            _store_rows(m_ref, h, qp, m_news[i])
            _store_rows(l_ref, h, qp, l_new)
            _store_rows(acc_ref, h, qp, acc_new)


def _attn_group(d, first, q_ref, k_ref, v_ref, bias_ref, acc_ref, m_ref, l_ref, stage_refs):
    if d == 16:
        def widen(c, _):
            rows = pl.ds(pl.multiple_of(c * ROW_CHUNK, ROW_CHUNK), ROW_CHUNK)
            for src_ref, dst_ref in zip((q_ref, k_ref, v_ref), stage_refs):
                for h in range(HEADS_PER_GROUP):
                    for b in range(CLS):
                        dst_ref[h, b, rows, :] = src_ref[h, b, rows, :].astype(F32)
            return _
        lax.fori_loop(0, CLS_LEN // ROW_CHUNK, widen, None)
        q_ref, k_ref, v_ref = stage_refs
    block = functools.partial(_attn_block, first=first, q_ref=q_ref, k_ref=k_ref, v_ref=v_ref,
                              acc_ref=acc_ref, m_ref=m_ref, l_ref=l_ref)
    if d == 16:
        for r in range(d):
            a, b = divmod(r, CLS)
            rows = [(b, pl.ds(a, BLK, stride=CLS))]
            block(rows, rows, bias_ref[0, :, BLK:])
    elif d == 4:
        for b in range(CLS):
            for bi in range(CLS_LEN // BLK):
                q0 = bi * BLK
                if bi == 0:
                    block([(b, pl.ds(q0, BLK))], [(b, pl.ds(q0, BLK))], bias_ref[0, :, BLK:])
                else:
                    block([(b, pl.ds(q0, BLK))], [(b, pl.ds(q0 - BLK, 2 * BLK))], bias_ref[0])
    else:
        piece = BLK // CLS
        for bi in range(SEQ // BLK):
            qp = [(b, pl.ds(bi * piece, piece)) for b in range(CLS)]
            if bi == 0:
                block(qp, qp, bias_ref[2, :, :BLK])
            else:
                kp = [(b, pl.ds((bi - 1) * piece, 2 * piece)) for b in range(CLS)]
                block(qp, kp, bias_ref[1])


def _attn_kernel(q_ref, k_ref, v_ref, z_ref, o_ref, bias_ref, acc_ref, m_ref, l_ref, nat_ref,
                 sq_ref, sk_ref, sv_ref):
    step = pl.program_id(1)
    _band_biases(bias_ref)

    for s in range(N_GROUPS):
        @pl.when(step == s)
        def _(s=s):
            _attn_group(ATTN_PATTERNS[N_GROUPS - 1 - s][1], s == 0, q_ref, k_ref, v_ref,
                        bias_ref, acc_ref, m_ref, l_ref, (sq_ref, sk_ref, sv_ref))

    @pl.when(step == N_GROUPS - 1)
    def _():
        half = CLS_LEN // 2
        for h in range(HEADS_PER_GROUP):
            for b in range(CLS):
                for k in range(2):
                    src = pl.ds(k * half, half)
                    nat_ref[h, pl.ds(b + k * half * CLS, half, stride=CLS), :] = (
                        acc_ref[h, b, src, :] / l_ref[h, b, src, :]
                        * _silu(z_ref[h, b, src, :].astype(F32)))

        def fin(c, _):
            rows = pl.ds(pl.multiple_of(c * ROW_CHUNK, ROW_CHUNK), ROW_CHUNK)
            for h in range(HEADS_PER_GROUP):
                o_ref[rows, h * HEAD_DIM:(h + 1) * HEAD_DIM] = nat_ref[h, rows, :].astype(o_ref.dtype)
            return _
        lax.fori_loop(0, SEQ // ROW_CHUNK, fin, None)


def _attn(qkv):
    nb = qkv.shape[1]
    n = nb * SEQ

    def head_spec(kind):
        return pl.BlockSpec((HEADS_PER_GROUP, None, CLS, CLS_LEN, HEAD_DIM),
                            lambda i, s: (kind * N_GROUPS + N_GROUPS - 1 - s, i, 0, 0, 0))

    slab = pltpu.VMEM((HEADS_PER_GROUP, CLS, CLS_LEN, HEAD_DIM), F32)
    return pl.pallas_call(
        _attn_kernel,
        grid=(nb, N_GROUPS),
        in_specs=[
            head_spec(0), head_spec(1), head_spec(2),
            pl.BlockSpec((HEADS_PER_GROUP, None, CLS, CLS_LEN, HEAD_DIM),
                         lambda i, s: (N_QKV_HEADS // HEADS_PER_GROUP, i, 0, 0, 0)),
        ],
        out_specs=pl.BlockSpec((SEQ, ATTN_OUT), lambda i, s: (i, 0)),
        out_shape=jax.ShapeDtypeStruct((n, ATTN_OUT), BF16),
        scratch_shapes=[pltpu.VMEM((3, BLK, 2 * BLK), F32), slab, slab, slab,
                        pltpu.VMEM((HEADS_PER_GROUP, SEQ, HEAD_DIM), F32), slab, slab, slab],
        compiler_params=pltpu.CompilerParams(
            dimension_semantics=("arbitrary", "arbitrary"),
            vmem_limit_bytes=VMEM_LIMIT_BYTES),
        name="attn",
    )(qkv, qkv, qkv, qkv)


def _merge_kernel(a_ref, cb_ref, cc_ref, cv_ref, zc_ref, hc_ref, hv_ref, mq_ref, zm_ref,
                  g0a_ref, g0b_ref, g1a_ref, g1b_ref, g2a_ref, g2b_ref,
                  x_ref, mk_ref, mv_ref, cw_ref, mqn_ref,
                  wa_ref, wc_ref, wm_ref, wo_ref, o_ref, u_ref, mg_ref, mgp_ref, *, n_tiles):
    ts = MERGE_TS
    tiles_per_seq = SEQ // ts
    t = pl.program_id(0)
    seq_start = (jnp.minimum(t, n_tiles - 1) % tiles_per_seq) == 0

    @pl.when(t == 0)
    def _():
        mg_ref[...] = jnp.zeros_like(mg_ref)

    mgp_ref[...] = mg_ref[...]
    o_ref[...] = x_ref[...] + jnp.dot(mgp_ref[...], wo_ref[...], preferred_element_type=F32)

    u = cc_ref[...].astype(F32) * cv_ref[...].astype(F32)
    uh = hc_ref[...].astype(F32) * hv_ref[...].astype(F32)
    u_ref[0:HALO, :] = jnp.where(seq_start, 0.0, uh)
    u_ref[HALO:HALO + ts, :] = u
    y = (cw_ref[0:1, :] * u
         + cw_ref[1:2, :] * u_ref[HALO - 1:HALO - 1 + ts, :]
         + cw_ref[2:3, :] * u_ref[HALO - 2:HALO - 2 + ts, :])
    c = (cb_ref[...].astype(F32) * y * _silu(zc_ref[...].astype(F32))).astype(BF16)

    gq = mqn_ref[...] * (MEM_HEAD_DIM ** -0.5)
    mo_parts = []
    for h in range(MEM_HEADS):
        cols = slice(h * MEM_HEAD_DIM, (h + 1) * MEM_HEAD_DIM)
        qh = _rms(mq_ref[:, cols].astype(F32), gq).astype(BF16)
        s = lax.dot_general(qh, mk_ref[:, cols], (((1,), (1,)), ((), ())),
                            preferred_element_type=F32)
        p = jnp.exp(s - jnp.max(s, axis=-1, keepdims=True))
        den = jnp.sum(p, axis=-1, keepdims=True)
        oh = jnp.dot(p.astype(BF16), mv_ref[:, cols], preferred_element_type=F32) / den
        mo_parts.append((oh * _silu(zm_ref[:, cols].astype(F32))).astype(BF16))
    mo = jnp.concatenate(mo_parts, axis=-1)

    ya = jnp.dot(a_ref[...], wa_ref[...], preferred_element_type=F32)
    yc = jnp.dot(c, wc_ref[...], preferred_element_type=F32)
    ym = jnp.dot(mo, wm_ref[...], preferred_element_type=F32)

    half = D_MODEL // 2
    for part, (ga, gc, gm) in enumerate(((g0a_ref, g1a_ref, g2a_ref), (g0b_ref, g1b_ref, g2b_ref))):
        cols = slice(part * half, (part + 1) * half)
        mg_ref[:, cols] = (jax.nn.sigmoid(ga[...].astype(F32)) * ya[:, cols]
                           + jax.nn.sigmoid(gc[...].astype(F32)) * yc[:, cols]
                           + jax.nn.sigmoid(gm[...].astype(F32)) * ym[:, cols]).astype(BF16)


def _merge(a, rest, x2, mk, mv, conv_w, mqn, wa, wc, wm, wo):
    n = x2.shape[0]
    ts = MERGE_TS
    n_tiles = n // ts
    tiles_per_seq = SEQ // ts
    half = D_MODEL // 2

    def cur(t):
        return jnp.minimum(t, n_tiles - 1)

    def prev(t):
        return jnp.maximum(t - 1, 0)

    def col_spec(col, width):
        return pl.BlockSpec((ts, width), lambda t: (cur(t), (col - REST_COL0) // width))

    def halo_spec(col):
        return pl.BlockSpec((HALO, CONV_WIDTH),
                            lambda t: (jnp.maximum(cur(t) * (ts // HALO) - 1, 0),
                                       (col - REST_COL0) // CONV_WIDTH))

    def const_spec(shape):
        return pl.BlockSpec(shape, lambda t: (0,) * len(shape), pipeline_mode=pl.Buffered(1))

    mem_spec = pl.BlockSpec((None, MEM_LEN, MEM_W), lambda t: (cur(t) // tiles_per_seq, 0, 0))
    in_specs = [
        pl.BlockSpec((ts, ATTN_OUT), lambda t: (cur(t), 0)),
        col_spec(COL_CB, CONV_WIDTH), col_spec(COL_CC, CONV_WIDTH),
        col_spec(COL_CV, CONV_WIDTH), col_spec(COL_ZC, CONV_WIDTH),
        halo_spec(COL_CC), halo_spec(COL_CV),
        col_spec(COL_MQ, MEM_W), col_spec(COL_ZM, MEM_W),
    ] + [col_spec(COL_G + k * half, half) for k in range(2 * N_BRANCH)] + [
        pl.BlockSpec((ts, D_MODEL), lambda t: (prev(t), 0)),
        mem_spec, mem_spec,
        const_spec((CONV_K, CONV_WIDTH)), const_spec((1, MEM_HEAD_DIM)),
        const_spec((ATTN_OUT, D_MODEL)), const_spec((CONV_WIDTH, D_MODEL)),
        const_spec((MEM_W, D_MODEL)), const_spec((D_MODEL, D_MODEL)),
    ]
    n_rest = 8 + 2 * N_BRANCH
    return pl.pallas_call(
        functools.partial(_merge_kernel, n_tiles=n_tiles),
        grid=(n_tiles + 1,),
        in_specs=in_specs,
        out_specs=pl.BlockSpec((ts, D_MODEL), lambda t: (prev(t), 0)),
        out_shape=jax.ShapeDtypeStruct((n, D_MODEL), F32),
        scratch_shapes=[pltpu.VMEM((HALO + ts, CONV_WIDTH), F32),
                        pltpu.VMEM((ts, D_MODEL), BF16), pltpu.VMEM((ts, D_MODEL), BF16)],
        compiler_params=pltpu.CompilerParams(
            dimension_semantics=("arbitrary",),
            vmem_limit_bytes=VMEM_LIMIT_BYTES),
        name="merge",
    )(a, *([rest] * n_rest), x2, mk, mv, conv_w, mqn, wa, wc, wm, wo)


def kernel(x, mem, norm_g, mem_norm_g, w_in, attn_q_norm, attn_k_norm, conv_w, mem_w_kv,
           mem_q_norm, mem_k_norm, w_br_attn, w_br_conv, w_br_mem, w_out):
    b, s, d = x.shape
    assert (s, d) == (SEQ, D_MODEL) and w_in.shape == (D_MODEL, IN_COLS)
    x2 = x.reshape(b * s, d)
    scale = HEAD_DIM ** -0.5
    head_gains = jnp.concatenate([jnp.repeat(attn_q_norm * scale, HEADS_PER_GROUP, axis=0),
                                  jnp.repeat(attn_k_norm, HEADS_PER_GROUP, axis=0),
                                  jnp.ones(((N_HEAD_TILES - N_NORM_TILES) * HEADS_PER_TILE, HEAD_DIM), F32)],
                                 axis=0)
    qkv, rest = _in_proj(x2, norm_g.reshape(1, d), head_gains, w_in.astype(BF16))
    mk, mv = _mem_kv(mem, mem_norm_g.reshape(1, d), mem_w_kv.astype(BF16),
                     mem_k_norm.reshape(1, MEM_HEAD_DIM))
    a = _attn(qkv)
    out = _merge(a, rest, x2, mk, mv, conv_w, mem_q_norm.reshape(1, MEM_HEAD_DIM),
                 w_br_attn.astype(BF16), w_br_conv.astype(BF16), w_br_mem.astype(BF16),
                 w_out.astype(BF16))
    return out.reshape(b, s, d)
```

```python
import functools

import jax
import jax.numpy as jnp
from jax import lax
from jax.experimental import pallas as pl
from jax.experimental.pallas import tpu as pltpu

D_MODEL = 2048
SEQ = 2048
HEAD_DIM = 128
ATTN_PATTERNS = ((128, 1), (512, 4), (2048, 16))
N_GROUPS = len(ATTN_PATTERNS)
HEADS_PER_GROUP = 4
GROUP_W = HEADS_PER_GROUP * HEAD_DIM
ATTN_QKV = N_GROUPS * GROUP_W
ATTN_OUT = GROUP_W
BLK = 128
CONV_WIDTH = 1024
CONV_K = 3
MEM_LEN = 256
MEM_HEADS = 4
MEM_HEAD_DIM = 256
MEM_W = MEM_HEADS * MEM_HEAD_DIM
N_BRANCH = 3
EPS = 1e-6
LOG2E = 1.4426950408889634
IN_COLS = 3 * ATTN_QKV + ATTN_OUT + 4 * CONV_WIDTH + 2 * MEM_W + N_BRANCH * D_MODEL

COL_Q = 0
COL_K = ATTN_QKV
COL_V = 2 * ATTN_QKV
COL_ZA = 3 * ATTN_QKV
COL_CB = COL_ZA + ATTN_OUT
COL_CC = COL_CB + CONV_WIDTH
COL_CV = COL_CC + CONV_WIDTH
COL_ZC = COL_CV + CONV_WIDTH
COL_MQ = COL_ZC + CONV_WIDTH
COL_ZM = COL_MQ + MEM_W
COL_G = COL_ZM + MEM_W

F32 = jnp.float32
BF16 = jnp.bfloat16

VMEM_LIMIT_BYTES = 56 * 1024 * 1024

PROJ_TM = 1024
PROJ_TN = 1024
MERGE_TS = 256
HALO = 16
ROW_CHUNK = 256


def _rms(t, gain):
    return t * lax.rsqrt(jnp.mean(t * t, axis=-1, keepdims=True) + EPS) * gain


def _silu(z):
    return z * jax.nn.sigmoid(z)


CLS = 4
CLS_LEN = SEQ // CLS
HEADS_PER_TILE = PROJ_TN // HEAD_DIM
N_QKV_HEADS = 3 * N_GROUPS * HEADS_PER_GROUP
N_NORM_HEADS = 2 * N_GROUPS * HEADS_PER_GROUP
N_HEAD_TILES = (COL_ZA + ATTN_OUT) // PROJ_TN
N_NORM_TILES = N_NORM_HEADS // HEADS_PER_TILE
REST_COL0 = N_HEAD_TILES * PROJ_TN
REST_COLS = IN_COLS - REST_COL0
assert N_NORM_HEADS % HEADS_PER_TILE == 0 and SEQ % PROJ_TM == 0 and PROJ_TM % CLS == 0
assert COL_ZA == N_QKV_HEADS * HEAD_DIM and (COL_ZA + ATTN_OUT) % PROJ_TN == 0
SILU_TILES = (COL_ZC // PROJ_TN, COL_ZM // PROJ_TN)
MQ_TILE = COL_MQ // PROJ_TN
assert CONV_WIDTH == PROJ_TN and MEM_W == PROJ_TN and COL_ZC % PROJ_TN == 0 and COL_MQ % PROJ_TN == 0


def _in_proj_kernel(x_ref, g_ref, hg_ref, mg_ref, w_ref, qkv_ref, o_ref, h_ref, hp_ref, tmp_ref):
    j = pl.program_id(1)
    rows_per_cls = PROJ_TM // CLS

    @pl.when(j == 0)
    def _():
        def body(c, _):
            rows = pl.ds(pl.multiple_of(c * ROW_CHUNK, ROW_CHUNK), ROW_CHUNK)
            hf = _rms(x_ref[rows, :], g_ref[...])
            h_ref[rows, :] = hf.astype(BF16)
            piece = ROW_CHUNK // CLS
            for s in range(D_MODEL // HEAD_DIM):
                lanes = slice(s * HEAD_DIM, (s + 1) * HEAD_DIM)
                tmp_ref[s] = hf[:, lanes]
                for b in range(CLS):
                    dst = pl.ds(pl.multiple_of(b * rows_per_cls + c * piece, piece), piece)
                    hp_ref[dst, lanes] = tmp_ref[s, pl.ds(b, piece, stride=CLS), :].astype(BF16)
            return _
        lax.fori_loop(0, PROJ_TM // ROW_CHUNK, body, None)

    def emit_heads(normed):
        res = jnp.dot(hp_ref[...], w_ref[...], preferred_element_type=F32)
        gains = hg_ref[pl.ds(pl.multiple_of(j * HEADS_PER_TILE, HEADS_PER_TILE), HEADS_PER_TILE), :]
        for s in range(HEADS_PER_TILE):
            r = res[:, s * HEAD_DIM:(s + 1) * HEAD_DIM]
            if normed:
                r = _rms(r, gains[s:s + 1, :])
            for b in range(CLS):
                qkv_ref[s, b] = r[b * rows_per_cls:(b + 1) * rows_per_cls].astype(qkv_ref.dtype)

    @pl.when(j < N_NORM_TILES)
    def _():
        emit_heads(True)

    @pl.when((j >= N_NORM_TILES) & (j < N_HEAD_TILES))
    def _():
        emit_heads(False)

    def project():
        return jnp.dot(h_ref[...], w_ref[...], preferred_element_type=F32)

    is_silu = (j == SILU_TILES[0]) | (j == SILU_TILES[1])

    @pl.when(is_silu)
    def _():
        o_ref[...] = _silu(project()).astype(o_ref.dtype)

    @pl.when(j == MQ_TILE)
    def _():
        res = project()
        for h in range(MEM_HEADS):
            cols = slice(h * MEM_HEAD_DIM, (h + 1) * MEM_HEAD_DIM)
            o_ref[:, cols] = _rms(res[:, cols], mg_ref[...]).astype(o_ref.dtype)

    @pl.when((j >= N_HEAD_TILES) & jnp.logical_not(is_silu) & (j != MQ_TILE))
    def _():
        o_ref[...] = project().astype(o_ref.dtype)


def _in_proj(x2, g, head_gains, mem_q_gain, w):
    n = x2.shape[0]
    tiles_per_seq = SEQ // PROJ_TM
    rows = PROJ_TM // CLS
    return pl.pallas_call(
        _in_proj_kernel,
        grid=(n // PROJ_TM, IN_COLS // PROJ_TN),
        in_specs=[
            pl.BlockSpec((PROJ_TM, D_MODEL), lambda i, j: (i, 0)),
            pl.BlockSpec((1, D_MODEL), lambda i, j: (0, 0)),
            pl.BlockSpec((N_HEAD_TILES * HEADS_PER_TILE, HEAD_DIM), lambda i, j: (0, 0)),
            pl.BlockSpec((1, MEM_HEAD_DIM), lambda i, j: (0, 0)),
            pl.BlockSpec((D_MODEL, PROJ_TN), lambda i, j: (0, j)),
        ],
        out_specs=[
            pl.BlockSpec((HEADS_PER_TILE, None, CLS, rows, HEAD_DIM),
                         lambda i, j: (jnp.minimum(j, N_HEAD_TILES - 1), i // tiles_per_seq, 0,
                                       i % tiles_per_seq, 0)),
            pl.BlockSpec((PROJ_TM, PROJ_TN),
                         lambda i, j: (i, jnp.maximum(j, N_HEAD_TILES) - N_HEAD_TILES)),
        ],
        out_shape=[
            jax.ShapeDtypeStruct((N_HEAD_TILES * HEADS_PER_TILE, n // SEQ, CLS, CLS_LEN, HEAD_DIM), BF16),
            jax.ShapeDtypeStruct((n, REST_COLS), BF16),
        ],
        scratch_shapes=[pltpu.VMEM((PROJ_TM, D_MODEL), BF16), pltpu.VMEM((PROJ_TM, D_MODEL), BF16),
                        pltpu.VMEM((D_MODEL // HEAD_DIM, ROW_CHUNK, HEAD_DIM), F32)],
        compiler_params=pltpu.CompilerParams(
            dimension_semantics=("arbitrary", "arbitrary"),
            vmem_limit_bytes=VMEM_LIMIT_BYTES),
        name="in_proj",
    )(x2, g, head_gains, mem_q_gain, w)


def _mem_kv_kernel(mem_ref, g_ref, w_ref, kn_ref, mk_ref, mv_ref):
    mh = _rms(mem_ref[...], g_ref[...]).astype(BF16)
    kv = jnp.dot(mh, w_ref[...], preferred_element_type=F32)
    for h in range(MEM_HEADS):
        cols = slice(h * MEM_HEAD_DIM, (h + 1) * MEM_HEAD_DIM)
        mk_ref[:, cols] = _rms(kv[:, cols], kn_ref[...]).astype(BF16)
    mv_ref[...] = kv[:, MEM_W:].astype(BF16)


def _mem_kv(mem, g, w, kn):
    b = mem.shape[0]
    return pl.pallas_call(
        _mem_kv_kernel,
        grid=(b,),
        in_specs=[
            pl.BlockSpec((None, MEM_LEN, D_MODEL), lambda i: (i, 0, 0)),
            pl.BlockSpec((1, D_MODEL), lambda i: (0, 0)),
            pl.BlockSpec((D_MODEL, 2 * MEM_W), lambda i: (0, 0)),
            pl.BlockSpec((1, MEM_HEAD_DIM), lambda i: (0, 0)),
        ],
        out_specs=[
            pl.BlockSpec((None, MEM_LEN, MEM_W), lambda i: (i, 0, 0)),
            pl.BlockSpec((None, MEM_LEN, MEM_W), lambda i: (i, 0, 0)),
        ],
        out_shape=[jax.ShapeDtypeStruct((b, MEM_LEN, MEM_W), BF16)] * 2,
        compiler_params=pltpu.CompilerParams(
            dimension_semantics=("arbitrary",),
            vmem_limit_bytes=VMEM_LIMIT_BYTES),
        name="mem_kv",
    )(mem, g, w, kn)


def _band_biases(bias_ref):
    r = lax.broadcasted_iota(jnp.int32, (BLK, 2 * BLK), 0)
    c = lax.broadcasted_iota(jnp.int32, (BLK, 2 * BLK), 1)

    def put(idx, diff):
        in_band = lax.bitcast_convert_type(diff, jnp.uint32) <= jnp.uint32(BLK)
        bias_ref[idx] = jnp.where(in_band, 0.0, -jnp.inf)

    put(0, c - r)
    piece = BLK // CLS
    b, jq = r // piece, r % piece
    bk, jk = c // (2 * piece), c % (2 * piece)
    put(1, CLS * (piece + jq - jk) + (b - bk))
    bk, jk = c // piece, c % piece
    put(2, CLS * (jq - jk) + (b - bk))


def _load_rows(ref, h, pieces):
    parts = [ref[h, b, rows, :] for b, rows in pieces]
    return parts[0] if len(parts) == 1 else jnp.concatenate(parts, axis=0)


def _store_rows(ref, h, pieces, val):
    off = 0
    for b, rows in pieces:
        ref[h, b, rows, :] = val[off:off + rows.size]
        off += rows.size


def _attn_block(qp, kp, bias, first, q_ref, k_ref, v_ref, acc_ref, m_ref, l_ref):
    nk = bias.shape[1]
    ones = jnp.ones((nk, HEAD_DIM), BF16)
    heads_per_dot = 2 * BLK // nk
    zeros = jnp.zeros((nk, HEAD_DIM), BF16)
    for h0 in range(0, HEADS_PER_GROUP, heads_per_dot):
        hs = range(h0, h0 + heads_per_dot)
        qb = jnp.concatenate([_load_rows(q_ref, h, qp) for h in hs], axis=-1).astype(BF16)
        ks = [_load_rows(k_ref, h, kp).astype(BF16) for h in hs]
        vs = [_load_rows(v_ref, h, kp).astype(BF16) for h in hs]
        if heads_per_dot == 1:
            kb = ks[0]
            vb = jnp.concatenate([vs[0], ones], axis=-1)
            bias_w = bias
        else:
            kb = jnp.concatenate([jnp.concatenate([ks[0], zeros], axis=-1),
                                  jnp.concatenate([zeros, ks[1]], axis=-1)], axis=0)
            vb = jnp.concatenate([jnp.concatenate([vs[0], ones, zeros, zeros], axis=-1),
                                  jnp.concatenate([zeros, zeros, vs[1], ones], axis=-1)], axis=0)
            bias_w = jnp.concatenate([bias, bias], axis=-1)
        s = lax.dot_general(qb, kb, (((1,), (1,)), ((), ())), preferred_element_type=F32) + bias_w
        w = s.shape[1] // heads_per_dot
        m_olds, m_news = [], []
        for i, h in enumerate(hs):
            mb = jnp.max(s[:, i * w:(i + 1) * w], axis=-1, keepdims=True)
            if first:
                m_olds.append(None)
                m_news.append(jnp.broadcast_to(mb, (BLK, HEAD_DIM)))
            else:
                m_olds.append(_load_rows(m_ref, h, qp))
                m_news.append(jnp.maximum(m_olds[-1], mb))
        m_wide = jnp.concatenate([m for m in m_news for _ in range(w // HEAD_DIM)], axis=-1)
        p = jnp.exp2(s - m_wide).astype(BF16)
        pv = jnp.dot(p, vb, preferred_element_type=F32)
        for i, h in enumerate(hs):
            acc_new = pv[:, 2 * i * HEAD_DIM:(2 * i + 1) * HEAD_DIM]
            l_new = pv[:, (2 * i + 1) * HEAD_DIM:(2 * i + 2) * HEAD_DIM]
            if not first:
                alpha = jnp.exp2(m_olds[i] - m_news[i])
                acc_new = alpha * _load_rows(acc_ref, h, qp) + acc_new
                l_new = alpha * _load_rows(l_ref, h, qp) + l_new
            _store_rows(m_ref, h, qp, m_news[i])
            _store_rows(l_ref, h, qp, l_new)
            _store_rows(acc_ref, h, qp, acc_new)


def _attn_group(d, first, q_ref, k_ref, v_ref, bias_ref, acc_ref, m_ref, l_ref, stage_refs):
    if d == 16:
        def widen(c, _):
            rows = pl.ds(pl.multiple_of(c * ROW_CHUNK, ROW_CHUNK), ROW_CHUNK)
            for src_ref, dst_ref in zip((q_ref, k_ref, v_ref), stage_refs):
                for h in range(HEADS_PER_GROUP):
                    for b in range(CLS):
                        dst_ref[h, b, rows, :] = src_ref[h, b, rows, :].astype(F32)
            return _
        lax.fori_loop(0, CLS_LEN // ROW_CHUNK, widen, None)
        q_ref, k_ref, v_ref = stage_refs
    block = functools.partial(_attn_block, first=first, q_ref=q_ref, k_ref=k_ref, v_ref=v_ref,
                              acc_ref=acc_ref, m_ref=m_ref, l_ref=l_ref)
    if d == 16:
        for r in range(d):
            a, b = divmod(r, CLS)
            rows = [(b, pl.ds(a, BLK, stride=CLS))]
            block(rows, rows, bias_ref[0, :, BLK:])
    elif d == 4:
        for b in range(CLS):
            for bi in range(CLS_LEN // BLK):
                q0 = bi * BLK
                if bi == 0:
                    block([(b, pl.ds(q0, BLK))], [(b, pl.ds(q0, BLK))], bias_ref[0, :, BLK:])
                else:
                    block([(b, pl.ds(q0, BLK))], [(b, pl.ds(q0 - BLK, 2 * BLK))], bias_ref[0])
    else:
        piece = BLK // CLS
        for bi in range(SEQ // BLK):
            qp = [(b, pl.ds(bi * piece, piece)) for b in range(CLS)]
            if bi == 0:
                block(qp, qp, bias_ref[2, :, :BLK])
            else:
                kp = [(b, pl.ds((bi - 1) * piece, 2 * piece)) for b in range(CLS)]
                block(qp, kp, bias_ref[1])


def _attn_kernel(q_ref, k_ref, v_ref, z_ref, o_ref, bias_ref, acc_ref, m_ref, l_ref, nat_ref,
                 sq_ref, sk_ref, sv_ref):
    step = pl.program_id(1)
    _band_biases(bias_ref)

    for s in range(N_GROUPS):
        @pl.when(step == s)
        def _(s=s):
            _attn_group(ATTN_PATTERNS[N_GROUPS - 1 - s][1], s == 0, q_ref, k_ref, v_ref,
                        bias_ref, acc_ref, m_ref, l_ref, (sq_ref, sk_ref, sv_ref))

    @pl.when(step == N_GROUPS - 1)
    def _():
        half = CLS_LEN // 2
        for h in range(HEADS_PER_GROUP):
            for b in range(CLS):
                for k in range(2):
                    src = pl.ds(k * half, half)
                    nat_ref[h, pl.ds(b + k * half * CLS, half, stride=CLS), :] = (
                        acc_ref[h, b, src, :] / l_ref[h, b, src, :]
                        * _silu(z_ref[h, b, src, :].astype(F32)))

        def fin(c, _):
            rows = pl.ds(pl.multiple_of(c * ROW_CHUNK, ROW_CHUNK), ROW_CHUNK)
            for h in range(HEADS_PER_GROUP):
                o_ref[rows, h * HEAD_DIM:(h + 1) * HEAD_DIM] = nat_ref[h, rows, :].astype(o_ref.dtype)
            return _
        lax.fori_loop(0, SEQ // ROW_CHUNK, fin, None)


def _attn(qkv):
    nb = qkv.shape[1]
    n = nb * SEQ

    def head_spec(kind):
        return pl.BlockSpec((HEADS_PER_GROUP, None, CLS, CLS_LEN, HEAD_DIM),
                            lambda i, s: (kind * N_GROUPS + N_GROUPS - 1 - s, i, 0, 0, 0))

    slab = pltpu.VMEM((HEADS_PER_GROUP, CLS, CLS_LEN, HEAD_DIM), F32)
    return pl.pallas_call(
        _attn_kernel,
        grid=(nb, N_GROUPS),
        in_specs=[
            head_spec(0), head_spec(1), head_spec(2),
            pl.BlockSpec((HEADS_PER_GROUP, None, CLS, CLS_LEN, HEAD_DIM),
                         lambda i, s: (N_QKV_HEADS // HEADS_PER_GROUP, i, 0, 0, 0)),
        ],
        out_specs=pl.BlockSpec((SEQ, ATTN_OUT), lambda i, s: (i, 0)),
        out_shape=jax.ShapeDtypeStruct((n, ATTN_OUT), BF16),
        scratch_shapes=[pltpu.VMEM((3, BLK, 2 * BLK), F32), slab, slab, slab,
                        pltpu.VMEM((HEADS_PER_GROUP, SEQ, HEAD_DIM), F32), slab, slab, slab],
        compiler_params=pltpu.CompilerParams(
            dimension_semantics=("arbitrary", "arbitrary"),
            vmem_limit_bytes=VMEM_LIMIT_BYTES),
        name="attn",
    )(qkv, qkv, qkv, qkv)


def _merge_kernel(a_ref, cb_ref, cc_ref, cv_ref, zc_ref, hc_ref, hv_ref, mq_ref, zm_ref,
                  g0a_ref, g0b_ref, g1a_ref, g1b_ref, g2a_ref, g2b_ref,
                  x_ref, mk_ref, mv_ref, cw_ref,
                  wa_ref, wc_ref, wm_ref, wo_ref, o_ref, u_ref, mg_ref, mgp_ref, *, n_tiles):
    ts = MERGE_TS
    tiles_per_seq = SEQ // ts
    t = pl.program_id(0)
    seq_start = (jnp.minimum(t, n_tiles - 1) % tiles_per_seq) == 0

    @pl.when(t == 0)
    def _():
        mg_ref[...] = jnp.zeros_like(mg_ref)

    mgp_ref[...] = mg_ref[...]
    o_ref[...] = x_ref[...] + jnp.dot(mgp_ref[...], wo_ref[...], preferred_element_type=F32)

    u = cc_ref[...].astype(F32) * cv_ref[...].astype(F32)
    uh = hc_ref[...].astype(F32) * hv_ref[...].astype(F32)
    u_ref[0:HALO, :] = jnp.where(seq_start, 0.0, uh)
    u_ref[HALO:HALO + ts, :] = u
    y = (cw_ref[0:1, :] * u
         + cw_ref[1:2, :] * u_ref[HALO - 1:HALO - 1 + ts, :]
         + cw_ref[2:3, :] * u_ref[HALO - 2:HALO - 2 + ts, :])
    c = (cb_ref[...].astype(F32) * y * zc_ref[...].astype(F32)).astype(BF16)

    mo_parts = []
    for h in range(MEM_HEADS):
        cols = slice(h * MEM_HEAD_DIM, (h + 1) * MEM_HEAD_DIM)
        s = lax.dot_general(mq_ref[:, cols], mk_ref[:, cols], (((1,), (1,)), ((), ())),
                            preferred_element_type=F32)
        p = jnp.exp2(s - jnp.max(s, axis=-1, keepdims=True))
        den = jnp.sum(p, axis=-1, keepdims=True)
        oh = jnp.dot(p.astype(BF16), mv_ref[:, cols], preferred_element_type=F32) / den
        mo_parts.append((oh * zm_ref[:, cols].astype(F32)).astype(BF16))
    mo = jnp.concatenate(mo_parts, axis=-1)

    ya = jnp.dot(a_ref[...], wa_ref[...], preferred_element_type=F32)
    yc = jnp.dot(c, wc_ref[...], preferred_element_type=F32)
    ym = jnp.dot(mo, wm_ref[...], preferred_element_type=F32)

    half = D_MODEL // 2
    for part, (ga, gc, gm) in enumerate(((g0a_ref, g1a_ref, g2a_ref), (g0b_ref, g1b_ref, g2b_ref))):
        cols = slice(part * half, (part + 1) * half)
        mg_ref[:, cols] = (jax.nn.sigmoid(ga[...].astype(F32)) * ya[:, cols]
                           + jax.nn.sigmoid(gc[...].astype(F32)) * yc[:, cols]
                           + jax.nn.sigmoid(gm[...].astype(F32)) * ym[:, cols]).astype(BF16)


def _merge(a, rest, x2, mk, mv, conv_w, wa, wc, wm, wo):
    n = x2.shape[0]
    ts = MERGE_TS
    n_tiles = n // ts
    tiles_per_seq = SEQ // ts
    half = D_MODEL // 2

    def cur(t):
        return jnp.minimum(t, n_tiles - 1)

    def prev(t):
        return jnp.maximum(t - 1, 0)

    def col_spec(col, width):
        return pl.BlockSpec((ts, width), lambda t: (cur(t), (col - REST_COL0) // width))

    def halo_spec(col):
        return pl.BlockSpec((HALO, CONV_WIDTH),
                            lambda t: (jnp.maximum(cur(t) * (ts // HALO) - 1, 0),
                                       (col - REST_COL0) // CONV_WIDTH))

    def const_spec(shape):
        return pl.BlockSpec(shape, lambda t: (0,) * len(shape), pipeline_mode=pl.Buffered(1))

    mem_spec = pl.BlockSpec((None, MEM_LEN, MEM_W), lambda t: (cur(t) // tiles_per_seq, 0, 0))
    in_specs = [
        pl.BlockSpec((ts, ATTN_OUT), lambda t: (cur(t), 0)),
        col_spec(COL_CB, CONV_WIDTH), col_spec(COL_CC, CONV_WIDTH),
        col_spec(COL_CV, CONV_WIDTH), col_spec(COL_ZC, CONV_WIDTH),
        halo_spec(COL_CC), halo_spec(COL_CV),
        col_spec(COL_MQ, MEM_W), col_spec(COL_ZM, MEM_W),
    ] + [col_spec(COL_G + k * half, half) for k in range(2 * N_BRANCH)] + [
        pl.BlockSpec((ts, D_MODEL), lambda t: (prev(t), 0)),
        mem_spec, mem_spec,
        const_spec((CONV_K, CONV_WIDTH)),
        const_spec((ATTN_OUT, D_MODEL)), const_spec((CONV_WIDTH, D_MODEL)),
        const_spec((MEM_W, D_MODEL)), const_spec((D_MODEL, D_MODEL)),
    ]
    n_rest = 8 + 2 * N_BRANCH
    return pl.pallas_call(
        functools.partial(_merge_kernel, n_tiles=n_tiles),
        grid=(n_tiles + 1,),
        in_specs=in_specs,
        out_specs=pl.BlockSpec((ts, D_MODEL), lambda t: (prev(t), 0)),
        out_shape=jax.ShapeDtypeStruct((n, D_MODEL), F32),
        scratch_shapes=[pltpu.VMEM((HALO + ts, CONV_WIDTH), F32),
                        pltpu.VMEM((ts, D_MODEL), BF16), pltpu.VMEM((ts, D_MODEL), BF16)],
        compiler_params=pltpu.CompilerParams(
            dimension_semantics=("arbitrary",),
            vmem_limit_bytes=VMEM_LIMIT_BYTES),
        name="merge",
    )(a, *([rest] * n_rest), x2, mk, mv, conv_w, wa, wc, wm, wo)


def kernel(x, mem, norm_g, mem_norm_g, w_in, attn_q_norm, attn_k_norm, conv_w, mem_w_kv,
           mem_q_norm, mem_k_norm, w_br_attn, w_br_conv, w_br_mem, w_out):
    b, s, d = x.shape
    assert (s, d) == (SEQ, D_MODEL) and w_in.shape == (D_MODEL, IN_COLS)
    x2 = x.reshape(b * s, d)
    scale = HEAD_DIM ** -0.5 * LOG2E
    head_gains = jnp.concatenate([jnp.repeat(attn_q_norm * scale, HEADS_PER_GROUP, axis=0),
                                  jnp.repeat(attn_k_norm, HEADS_PER_GROUP, axis=0),
                                  jnp.ones(((N_HEAD_TILES - N_NORM_TILES) * HEADS_PER_TILE, HEAD_DIM), F32)],
                                 axis=0)
    mem_q_gain = mem_q_norm.reshape(1, MEM_HEAD_DIM) * (MEM_HEAD_DIM ** -0.5 * LOG2E)
    qkv, rest = _in_proj(x2, norm_g.reshape(1, d), head_gains, mem_q_gain, w_in.astype(BF16))
    mk, mv = _mem_kv(mem, mem_norm_g.reshape(1, d), mem_w_kv.astype(BF16),
                     mem_k_norm.reshape(1, MEM_HEAD_DIM))
    a = _attn(qkv)
    out = _merge(a, rest, x2, mk, mv, conv_w,
                 w_br_attn.astype(BF16), w_br_conv.astype(BF16), w_br_mem.astype(BF16),
                 w_out.astype(BF16))
    return out.reshape(b, s, d)
```

```python
import functools

import jax
import jax.numpy as jnp
from jax import lax
from jax.experimental import pallas as pl
from jax.experimental.pallas import tpu as pltpu

D_MODEL = 2048
SEQ = 2048
HEAD_DIM = 128
ATTN_PATTERNS = ((128, 1), (512, 4), (2048, 16))
N_GROUPS = len(ATTN_PATTERNS)
HEADS_PER_GROUP = 4
GROUP_W = HEADS_PER_GROUP * HEAD_DIM
ATTN_QKV = N_GROUPS * GROUP_W
ATTN_OUT = GROUP_W
BLK = 128
CONV_WIDTH = 1024
CONV_K = 3
MEM_LEN = 256
MEM_HEADS = 4
MEM_HEAD_DIM = 256
MEM_W = MEM_HEADS * MEM_HEAD_DIM
N_BRANCH = 3
EPS = 1e-6
LOG2E = 1.4426950408889634
IN_COLS = 3 * ATTN_QKV + ATTN_OUT + 4 * CONV_WIDTH + 2 * MEM_W + N_BRANCH * D_MODEL

COL_Q = 0
COL_K = ATTN_QKV
COL_V = 2 * ATTN_QKV
COL_ZA = 3 * ATTN_QKV
COL_CB = COL_ZA + ATTN_OUT
COL_CC = COL_CB + CONV_WIDTH
COL_CV = COL_CC + CONV_WIDTH
COL_ZC = COL_CV + CONV_WIDTH
COL_MQ = COL_ZC + CONV_WIDTH
COL_ZM = COL_MQ + MEM_W
COL_G = COL_ZM + MEM_W

F32 = jnp.float32
BF16 = jnp.bfloat16

VMEM_LIMIT_BYTES = 56 * 1024 * 1024

PROJ_TM = 1024
PROJ_TN = 1024
MERGE_TS = 256
HALO = 16
ROW_CHUNK = 256


def _rms(t, gain):
    return t * lax.rsqrt(jnp.mean(t * t, axis=-1, keepdims=True) + EPS) * gain


def _silu(z):
    return z * jax.nn.sigmoid(z)


CLS = 4
CLS_LEN = SEQ // CLS
HEADS_PER_TILE = PROJ_TN // HEAD_DIM
N_QKV_HEADS = 3 * N_GROUPS * HEADS_PER_GROUP
N_NORM_HEADS = 2 * N_GROUPS * HEADS_PER_GROUP
N_HEAD_TILES = (COL_ZA + ATTN_OUT) // PROJ_TN
N_NORM_TILES = N_NORM_HEADS // HEADS_PER_TILE
REST_COL0 = N_HEAD_TILES * PROJ_TN
REST_COLS = IN_COLS - REST_COL0
TOK_COLS = COL_G - REST_COL0
assert N_NORM_HEADS % HEADS_PER_TILE == 0 and SEQ % PROJ_TM == 0 and PROJ_TM % CLS == 0
assert COL_ZA == N_QKV_HEADS * HEAD_DIM and (COL_ZA + ATTN_OUT) % PROJ_TN == 0
SILU_TILES = (COL_ZC // PROJ_TN, COL_ZM // PROJ_TN)
MQ_TILE = COL_MQ // PROJ_TN
assert CONV_WIDTH == PROJ_TN and MEM_W == PROJ_TN and COL_ZC % PROJ_TN == 0 and COL_MQ % PROJ_TN == 0
assert TOK_COLS == N_BRANCH * D_MODEL == REST_COLS // 2 and COL_CB == REST_COL0


def _in_proj_kernel(x_ref, g_ref, hg_ref, mg_ref, w_ref, qkv_ref, o_ref, h_ref, hp_ref, tmp_ref):
    j = pl.program_id(1)
    rows_per_cls = PROJ_TM // CLS

    @pl.when(j == 0)
    def _():
        def body(c, _):
            rows = pl.ds(pl.multiple_of(c * ROW_CHUNK, ROW_CHUNK), ROW_CHUNK)
            hf = _rms(x_ref[rows, :], g_ref[...])
            h_ref[rows, :] = hf.astype(BF16)
            piece = ROW_CHUNK // CLS
            for s in range(D_MODEL // HEAD_DIM):
                lanes = slice(s * HEAD_DIM, (s + 1) * HEAD_DIM)
                tmp_ref[s] = hf[:, lanes]
                for b in range(CLS):
                    dst = pl.ds(pl.multiple_of(b * rows_per_cls + c * piece, piece), piece)
                    hp_ref[dst, lanes] = tmp_ref[s, pl.ds(b, piece, stride=CLS), :].astype(BF16)
            return _
        lax.fori_loop(0, PROJ_TM // ROW_CHUNK, body, None)

    def emit_heads(normed):
        res = jnp.dot(hp_ref[...], w_ref[...], preferred_element_type=F32)
        gains = hg_ref[pl.ds(pl.multiple_of(j * HEADS_PER_TILE, HEADS_PER_TILE), HEADS_PER_TILE), :]
        for s in range(HEADS_PER_TILE):
            r = res[:, s * HEAD_DIM:(s + 1) * HEAD_DIM]
            if normed:
                r = _rms(r, gains[s:s + 1, :])
            for b in range(CLS):
                qkv_ref[s, b] = r[b * rows_per_cls:(b + 1) * rows_per_cls].astype(qkv_ref.dtype)

    @pl.when(j < N_NORM_TILES)
    def _():
        emit_heads(True)

    @pl.when((j >= N_NORM_TILES) & (j < N_HEAD_TILES))
    def _():
        emit_heads(False)

    def project():
        return jnp.dot(h_ref[...], w_ref[...], preferred_element_type=F32)

    is_silu = (j == SILU_TILES[0]) | (j == SILU_TILES[1])

    @pl.when(is_silu)
    def _():
        o_ref[...] = _silu(project()).astype(o_ref.dtype)

    @pl.when(j == MQ_TILE)
    def _():
        res = project()
        for h in range(MEM_HEADS):
            cols = slice(h * MEM_HEAD_DIM, (h + 1) * MEM_HEAD_DIM)
            o_ref[:, cols] = _rms(res[:, cols], mg_ref[...]).astype(o_ref.dtype)

    @pl.when((j >= N_HEAD_TILES) & jnp.logical_not(is_silu) & (j != MQ_TILE))
    def _():
        o_ref[...] = project().astype(o_ref.dtype)


def _in_proj(x2, g, head_gains, mem_q_gain, w):
    n = x2.shape[0]
    tiles_per_seq = SEQ // PROJ_TM
    rows = PROJ_TM // CLS
    return pl.pallas_call(
        _in_proj_kernel,
        grid=(n // PROJ_TM, IN_COLS // PROJ_TN),
        in_specs=[
            pl.BlockSpec((PROJ_TM, D_MODEL), lambda i, j: (i, 0)),
            pl.BlockSpec((1, D_MODEL), lambda i, j: (0, 0)),
            pl.BlockSpec((N_HEAD_TILES * HEADS_PER_TILE, HEAD_DIM), lambda i, j: (0, 0)),
            pl.BlockSpec((1, MEM_HEAD_DIM), lambda i, j: (0, 0)),
            pl.BlockSpec((D_MODEL, PROJ_TN), lambda i, j: (0, j)),
        ],
        out_specs=[
            pl.BlockSpec((HEADS_PER_TILE, None, CLS, rows, HEAD_DIM),
                         lambda i, j: (jnp.minimum(j, N_HEAD_TILES - 1), i // tiles_per_seq, 0,
                                       i % tiles_per_seq, 0)),
            pl.BlockSpec((PROJ_TM, PROJ_TN),
                         lambda i, j: (i, jnp.maximum(j, N_HEAD_TILES) - N_HEAD_TILES)),
        ],
        out_shape=[
            jax.ShapeDtypeStruct((N_HEAD_TILES * HEADS_PER_TILE, n // SEQ, CLS, CLS_LEN, HEAD_DIM), BF16),
            jax.ShapeDtypeStruct((n, REST_COLS), BF16),
        ],
        scratch_shapes=[pltpu.VMEM((PROJ_TM, D_MODEL), BF16), pltpu.VMEM((PROJ_TM, D_MODEL), BF16),
                        pltpu.VMEM((D_MODEL // HEAD_DIM, ROW_CHUNK, HEAD_DIM), F32)],
        compiler_params=pltpu.CompilerParams(
            dimension_semantics=("arbitrary", "arbitrary"),
            vmem_limit_bytes=VMEM_LIMIT_BYTES),
        name="in_proj",
    )(x2, g, head_gains, mem_q_gain, w)


def _mem_kv_kernel(mem_ref, g_ref, w_ref, kn_ref, mk_ref, mv_ref):
    mh = _rms(mem_ref[...], g_ref[...]).astype(BF16)
    kv = jnp.dot(mh, w_ref[...], preferred_element_type=F32)
    for h in range(MEM_HEADS):
        cols = slice(h * MEM_HEAD_DIM, (h + 1) * MEM_HEAD_DIM)
        mk_ref[:, cols] = _rms(kv[:, cols], kn_ref[...]).astype(BF16)
    mv_ref[...] = kv[:, MEM_W:].astype(BF16)


def _mem_kv(mem, g, w, kn):
    b = mem.shape[0]
    return pl.pallas_call(
        _mem_kv_kernel,
        grid=(b,),
        in_specs=[
            pl.BlockSpec((None, MEM_LEN, D_MODEL), lambda i: (i, 0, 0)),
            pl.BlockSpec((1, D_MODEL), lambda i: (0, 0)),
            pl.BlockSpec((D_MODEL, 2 * MEM_W), lambda i: (0, 0)),
            pl.BlockSpec((1, MEM_HEAD_DIM), lambda i: (0, 0)),
        ],
        out_specs=[
            pl.BlockSpec((None, MEM_LEN, MEM_W), lambda i: (i, 0, 0)),
            pl.BlockSpec((None, MEM_LEN, MEM_W), lambda i: (i, 0, 0)),
        ],
        out_shape=[jax.ShapeDtypeStruct((b, MEM_LEN, MEM_W), BF16)] * 2,
        compiler_params=pltpu.CompilerParams(
            dimension_semantics=("arbitrary",),
            vmem_limit_bytes=VMEM_LIMIT_BYTES),
        name="mem_kv",
    )(mem, g, w, kn)


def _band_biases(bias_ref):
    r = lax.broadcasted_iota(jnp.int32, (BLK, 2 * BLK), 0)
    c = lax.broadcasted_iota(jnp.int32, (BLK, 2 * BLK), 1)

    def put(idx, diff):
        in_band = lax.bitcast_convert_type(diff, jnp.uint32) <= jnp.uint32(BLK)
        bias_ref[idx] = jnp.where(in_band, 0.0, -jnp.inf)

    put(0, c - r)
    piece = BLK // CLS
    b, jq = r // piece, r % piece
    bk, jk = c // (2 * piece), c % (2 * piece)
    put(1, CLS * (piece + jq - jk) + (b - bk))
    bk, jk = c // piece, c % piece
    put(2, CLS * (jq - jk) + (b - bk))


def _load_rows(ref, h, pieces):
    parts = [ref[h, b, rows, :] for b, rows in pieces]
    return parts[0] if len(parts) == 1 else jnp.concatenate(parts, axis=0)


def _store_rows(ref, h, pieces, val):
    off = 0
    for b, rows in pieces:
        ref[h, b, rows, :] = val[off:off + rows.size]
        off += rows.size


def _attn_block(qp, kp, bias, first, q_ref, k_ref, v_ref, acc_ref, m_ref, l_ref):
    nk = bias.shape[1]
    ones = jnp.ones((nk, HEAD_DIM), BF16)
    heads_per_dot = 2 * BLK // nk
    zeros = jnp.zeros((nk, HEAD_DIM), BF16)
    for h0 in range(0, HEADS_PER_GROUP, heads_per_dot):
        hs = range(h0, h0 + heads_per_dot)
        qb = jnp.concatenate([_load_rows(q_ref, h, qp) for h in hs], axis=-1).astype(BF16)
        ks = [_load_rows(k_ref, h, kp).astype(BF16) for h in hs]
        vs = [_load_rows(v_ref, h, kp).astype(BF16) for h in hs]
        if heads_per_dot == 1:
            kb = ks[0]
            vb = jnp.concatenate([vs[0], ones], axis=-1)
            bias_w = bias
        else:
            kb = jnp.concatenate([jnp.concatenate([ks[0], zeros], axis=-1),
                                  jnp.concatenate([zeros, ks[1]], axis=-1)], axis=0)
            vb = jnp.concatenate([jnp.concatenate([vs[0], ones, zeros, zeros], axis=-1),
                                  jnp.concatenate([zeros, zeros, vs[1], ones], axis=-1)], axis=0)
            bias_w = jnp.concatenate([bias, bias], axis=-1)
        s = lax.dot_general(qb, kb, (((1,), (1,)), ((), ())), preferred_element_type=F32) + bias_w
        w = s.shape[1] // heads_per_dot
        m_olds, m_news = [], []
        for i, h in enumerate(hs):
            mb = jnp.max(s[:, i * w:(i + 1) * w], axis=-1, keepdims=True)
            if first:
                m_olds.append(None)
                m_news.append(jnp.broadcast_to(mb, (BLK, HEAD_DIM)))
            else:
                m_olds.append(_load_rows(m_ref, h, qp))
                m_news.append(jnp.maximum(m_olds[-1], mb))
        m_wide = jnp.concatenate([m for m in m_news for _ in range(w // HEAD_DIM)], axis=-1)
        p = jnp.exp2(s - m_wide).astype(BF16)
        pv = jnp.dot(p, vb, preferred_element_type=F32)
        for i, h in enumerate(hs):
            acc_new = pv[:, 2 * i * HEAD_DIM:(2 * i + 1) * HEAD_DIM]
            l_new = pv[:, (2 * i + 1) * HEAD_DIM:(2 * i + 2) * HEAD_DIM]
            if not first:
                alpha = jnp.exp2(m_olds[i] - m_news[i])
                acc_new = alpha * _load_rows(acc_ref, h, qp) + acc_new
                l_new = alpha * _load_rows(l_ref, h, qp) + l_new
            _store_rows(m_ref, h, qp, m_news[i])
            _store_rows(l_ref, h, qp, l_new)
            _store_rows(acc_ref, h, qp, acc_new)


def _attn_group(d, first, q_ref, k_ref, v_ref, bias_ref, acc_ref, m_ref, l_ref, stage_refs):
    if d == 16:
        def widen(c, _):
            rows = pl.ds(pl.multiple_of(c * ROW_CHUNK, ROW_CHUNK), ROW_CHUNK)
            for src_ref, dst_ref in zip((q_ref, k_ref, v_ref), stage_refs):
                for h in range(HEADS_PER_GROUP):
                    for b in range(CLS):
                        dst_ref[h, b, rows, :] = src_ref[h, b, rows, :].astype(F32)
            return _
        lax.fori_loop(0, CLS_LEN // ROW_CHUNK, widen, None)
        q_ref, k_ref, v_ref = stage_refs
    block = functools.partial(_attn_block, first=first, q_ref=q_ref, k_ref=k_ref, v_ref=v_ref,
                              acc_ref=acc_ref, m_ref=m_ref, l_ref=l_ref)
    if d == 16:
        for r in range(d):
            a, b = divmod(r, CLS)
            rows = [(b, pl.ds(a, BLK, stride=CLS))]
            block(rows, rows, bias_ref[0, :, BLK:])
    elif d == 4:
        for b in range(CLS):
            for bi in range(CLS_LEN // BLK):
                q0 = bi * BLK
                if bi == 0:
                    block([(b, pl.ds(q0, BLK))], [(b, pl.ds(q0, BLK))], bias_ref[0, :, BLK:])
                else:
                    block([(b, pl.ds(q0, BLK))], [(b, pl.ds(q0 - BLK, 2 * BLK))], bias_ref[0])
    else:
        piece = BLK // CLS
        for bi in range(SEQ // BLK):
            qp = [(b, pl.ds(bi * piece, piece)) for b in range(CLS)]
            if bi == 0:
                block(qp, qp, bias_ref[2, :, :BLK])
            else:
                kp = [(b, pl.ds((bi - 1) * piece, 2 * piece)) for b in range(CLS)]
                block(qp, kp, bias_ref[1])


def _attn_kernel(q_ref, k_ref, v_ref, z_ref, o_ref, bias_ref, acc_ref, m_ref, l_ref, nat_ref,
                 sq_ref, sk_ref, sv_ref):
    step = pl.program_id(1)
    _band_biases(bias_ref)

    for s in range(N_GROUPS):
        @pl.when(step == s)
        def _(s=s):
            _attn_group(ATTN_PATTERNS[N_GROUPS - 1 - s][1], s == 0, q_ref, k_ref, v_ref,
                        bias_ref, acc_ref, m_ref, l_ref, (sq_ref, sk_ref, sv_ref))

    @pl.when(step == N_GROUPS - 1)
    def _():
        half = CLS_LEN // 2
        for h in range(HEADS_PER_GROUP):
            for b in range(CLS):
                for k in range(2):
                    src = pl.ds(k * half, half)
                    nat_ref[h, pl.ds(b + k * half * CLS, half, stride=CLS), :] = (
                        acc_ref[h, b, src, :] / l_ref[h, b, src, :]
                        * _silu(z_ref[h, b, src, :].astype(F32)))

        def fin(c, _):
            rows = pl.ds(pl.multiple_of(c * ROW_CHUNK, ROW_CHUNK), ROW_CHUNK)
            for h in range(HEADS_PER_GROUP):
                o_ref[rows, h * HEAD_DIM:(h + 1) * HEAD_DIM] = nat_ref[h, rows, :].astype(o_ref.dtype)
            return _
        lax.fori_loop(0, SEQ // ROW_CHUNK, fin, None)


def _attn(qkv):
    nb = qkv.shape[1]
    n = nb * SEQ

    def head_spec(kind):
        return pl.BlockSpec((HEADS_PER_GROUP, None, CLS, CLS_LEN, HEAD_DIM),
                            lambda i, s: (kind * N_GROUPS + N_GROUPS - 1 - s, i, 0, 0, 0))

    slab = pltpu.VMEM((HEADS_PER_GROUP, CLS, CLS_LEN, HEAD_DIM), F32)
    return pl.pallas_call(
        _attn_kernel,
        grid=(nb, N_GROUPS),
        in_specs=[
            head_spec(0), head_spec(1), head_spec(2),
            pl.BlockSpec((HEADS_PER_GROUP, None, CLS, CLS_LEN, HEAD_DIM),
                         lambda i, s: (N_QKV_HEADS // HEADS_PER_GROUP, i, 0, 0, 0)),
        ],
        out_specs=pl.BlockSpec((SEQ, ATTN_OUT), lambda i, s: (i, 0)),
        out_shape=jax.ShapeDtypeStruct((n, ATTN_OUT), BF16),
        scratch_shapes=[pltpu.VMEM((3, BLK, 2 * BLK), F32), slab, slab, slab,
                        pltpu.VMEM((HEADS_PER_GROUP, SEQ, HEAD_DIM), F32), slab, slab, slab],
        compiler_params=pltpu.CompilerParams(
            dimension_semantics=("arbitrary", "arbitrary"),
            vmem_limit_bytes=VMEM_LIMIT_BYTES),
        name="attn",
    )(qkv, qkv, qkv, qkv)


def _merge_kernel(a_ref, tok_ref, halo_ref, gate_ref, x_ref, mk_ref, mv_ref, cw_ref,
                  wa_ref, wc_ref, wm_ref, wo_ref, o_ref, u_ref, mg_ref, mgp_ref, *, n_tiles):
    ts = MERGE_TS
    tiles_per_seq = SEQ // ts
    t = pl.program_id(0)
    seq_start = (jnp.minimum(t, n_tiles - 1) % tiles_per_seq) == 0

    def piece(ref, col, width):
        return ref.at[:, pl.ds(col - REST_COL0, width)]

    cb_ref, cc_ref, cv_ref, zc_ref = (piece(tok_ref, c, CONV_WIDTH) for c in (COL_CB, COL_CC, COL_CV, COL_ZC))
    mq_ref, zm_ref = piece(tok_ref, COL_MQ, MEM_W), piece(tok_ref, COL_ZM, MEM_W)
    hc_ref, hv_ref = piece(halo_ref, COL_CC, CONV_WIDTH), piece(halo_ref, COL_CV, CONV_WIDTH)

    @pl.when(t == 0)
    def _():
        mg_ref[...] = jnp.zeros_like(mg_ref)

    mgp_ref[...] = mg_ref[...]
    o_ref[...] = x_ref[...] + jnp.dot(mgp_ref[...], wo_ref[...], preferred_element_type=F32)

    u = cc_ref[...].astype(F32) * cv_ref[...].astype(F32)
    uh = hc_ref[...].astype(F32) * hv_ref[...].astype(F32)
    u_ref[0:HALO, :] = jnp.where(seq_start, 0.0, uh)
    u_ref[HALO:HALO + ts, :] = u
    y = (cw_ref[0:1, :] * u
         + cw_ref[1:2, :] * u_ref[HALO - 1:HALO - 1 + ts, :]
         + cw_ref[2:3, :] * u_ref[HALO - 2:HALO - 2 + ts, :])
    c = (cb_ref[...].astype(F32) * y * zc_ref[...].astype(F32)).astype(BF16)

    mo_parts = []
    for h in range(MEM_HEADS):
        cols = slice(h * MEM_HEAD_DIM, (h + 1) * MEM_HEAD_DIM)
        s = lax.dot_general(mq_ref[:, cols], mk_ref[:, cols], (((1,), (1,)), ((), ())),
                            preferred_element_type=F32)
        p = jnp.exp2(s - jnp.max(s, axis=-1, keepdims=True))
        den = jnp.sum(p, axis=-1, keepdims=True)
        oh = jnp.dot(p.astype(BF16), mv_ref[:, cols], preferred_element_type=F32) / den
        mo_parts.append((oh * zm_ref[:, cols].astype(F32)).astype(BF16))
    mo = jnp.concatenate(mo_parts, axis=-1)

    ya = jnp.dot(a_ref[...], wa_ref[...], preferred_element_type=F32)
    yc = jnp.dot(c, wc_ref[...], preferred_element_type=F32)
    ym = jnp.dot(mo, wm_ref[...], preferred_element_type=F32)

    half = D_MODEL // 2
    for part in range(2):
        cols = slice(part * half, (part + 1) * half)
        ga, gc, gm = (gate_ref[:, k * D_MODEL + part * half:k * D_MODEL + (part + 1) * half]
                      for k in range(N_BRANCH))
        mg_ref[:, cols] = (jax.nn.sigmoid(ga.astype(F32)) * ya[:, cols]
                           + jax.nn.sigmoid(gc.astype(F32)) * yc[:, cols]
                           + jax.nn.sigmoid(gm.astype(F32)) * ym[:, cols]).astype(BF16)


def _merge(a, rest, x2, mk, mv, conv_w, wa, wc, wm, wo):
    n = x2.shape[0]
    ts = MERGE_TS
    n_tiles = n // ts
    tiles_per_seq = SEQ // ts

    def cur(t):
        return jnp.minimum(t, n_tiles - 1)

    def prev(t):
        return jnp.maximum(t - 1, 0)

    def const_spec(shape):
        return pl.BlockSpec(shape, lambda t: (0,) * len(shape), pipeline_mode=pl.Buffered(1))

    mem_spec = pl.BlockSpec((None, MEM_LEN, MEM_W), lambda t: (cur(t) // tiles_per_seq, 0, 0))
    in_specs = [
        pl.BlockSpec((ts, ATTN_OUT), lambda t: (cur(t), 0)),
        pl.BlockSpec((ts, TOK_COLS), lambda t: (cur(t), 0)),
        pl.BlockSpec((HALO, TOK_COLS), lambda t: (jnp.maximum(cur(t) * (ts // HALO) - 1, 0), 0)),
        pl.BlockSpec((ts, N_BRANCH * D_MODEL), lambda t: (cur(t), 1)),
        pl.BlockSpec((ts, D_MODEL), lambda t: (prev(t), 0)),
        mem_spec, mem_spec,
        const_spec((CONV_K, CONV_WIDTH)),
        const_spec((ATTN_OUT, D_MODEL)), const_spec((CONV_WIDTH, D_MODEL)),
        const_spec((MEM_W, D_MODEL)), const_spec((D_MODEL, D_MODEL)),
    ]
    return pl.pallas_call(
        functools.partial(_merge_kernel, n_tiles=n_tiles),
        grid=(n_tiles + 1,),
        in_specs=in_specs,
        out_specs=pl.BlockSpec((ts, D_MODEL), lambda t: (prev(t), 0)),
        out_shape=jax.ShapeDtypeStruct((n, D_MODEL), F32),
        scratch_shapes=[pltpu.VMEM((HALO + ts, CONV_WIDTH), F32),
                        pltpu.VMEM((ts, D_MODEL), BF16), pltpu.VMEM((ts, D_MODEL), BF16)],
        compiler_params=pltpu.CompilerParams(
            dimension_semantics=("arbitrary",),
            vmem_limit_bytes=VMEM_LIMIT_BYTES),
        name="merge",
    )(a, rest, rest, rest, x2, mk, mv, conv_w, wa, wc, wm, wo)


def kernel(x, mem, norm_g, mem_norm_g, w_in, attn_q_norm, attn_k_norm, conv_w, mem_w_kv,
           mem_q_norm, mem_k_norm, w_br_attn, w_br_conv, w_br_mem, w_out):
    b, s, d = x.shape
    assert (s, d) == (SEQ, D_MODEL) and w_in.shape == (D_MODEL, IN_COLS)
    x2 = x.reshape(b * s, d)
    scale = HEAD_DIM ** -0.5 * LOG2E
    head_gains = jnp.concatenate([jnp.repeat(attn_q_norm * scale, HEADS_PER_GROUP, axis=0),
                                  jnp.repeat(attn_k_norm, HEADS_PER_GROUP, axis=0),
                                  jnp.ones(((N_HEAD_TILES - N_NORM_TILES) * HEADS_PER_TILE, HEAD_DIM), F32)],
                                 axis=0)
    mem_q_gain = mem_q_norm.reshape(1, MEM_HEAD_DIM) * (MEM_HEAD_DIM ** -0.5 * LOG2E)
    qkv, rest = _in_proj(x2, norm_g.reshape(1, d), head_gains, mem_q_gain, w_in.astype(BF16))
    mk, mv = _mem_kv(mem, mem_norm_g.reshape(1, d), mem_w_kv.astype(BF16),
                     mem_k_norm.reshape(1, MEM_HEAD_DIM))
    a = _attn(qkv)
    out = _merge(a, rest, x2, mk, mv, conv_w,
                 w_br_attn.astype(BF16), w_br_conv.astype(BF16), w_br_mem.astype(BF16),
                 w_out.astype(BF16))
    return out.reshape(b, s, d)
```

```python
import functools

import jax
import jax.numpy as jnp
from jax import lax
from jax.experimental import pallas as pl
from jax.experimental.pallas import tpu as pltpu

D_MODEL = 2048
SEQ = 2048
HEAD_DIM = 128
ATTN_PATTERNS = ((128, 1), (512, 4), (2048, 16))
N_GROUPS = len(ATTN_PATTERNS)
HEADS_PER_GROUP = 4
GROUP_W = HEADS_PER_GROUP * HEAD_DIM
ATTN_QKV = N_GROUPS * GROUP_W
ATTN_OUT = GROUP_W
BLK = 128
CONV_WIDTH = 1024
CONV_K = 3
MEM_LEN = 256
MEM_HEADS = 4
MEM_HEAD_DIM = 256
MEM_W = MEM_HEADS * MEM_HEAD_DIM
N_BRANCH = 3
EPS = 1e-6
LOG2E = 1.4426950408889634
IN_COLS = 3 * ATTN_QKV + ATTN_OUT + 4 * CONV_WIDTH + 2 * MEM_W + N_BRANCH * D_MODEL

COL_Q = 0
COL_K = ATTN_QKV
COL_V = 2 * ATTN_QKV
COL_ZA = 3 * ATTN_QKV
COL_CB = COL_ZA + ATTN_OUT
COL_CC = COL_CB + CONV_WIDTH
COL_CV = COL_CC + CONV_WIDTH
COL_ZC = COL_CV + CONV_WIDTH
COL_MQ = COL_ZC + CONV_WIDTH
COL_ZM = COL_MQ + MEM_W
COL_G = COL_ZM + MEM_W

F32 = jnp.float32
BF16 = jnp.bfloat16

VMEM_LIMIT_BYTES = 56 * 1024 * 1024

PROJ_TM = 1024
PROJ_TN = 1024
MERGE_TS = 256
HALO = 16
ROW_CHUNK = 256


def _rms(t, gain):
    return t * lax.rsqrt(jnp.mean(t * t, axis=-1, keepdims=True) + EPS) * gain


def _silu(z):
    return z * jax.nn.sigmoid(z)


CLS = 4
CLS_LEN = SEQ // CLS
HEADS_PER_TILE = PROJ_TN // HEAD_DIM
N_QKV_HEADS = 3 * N_GROUPS * HEADS_PER_GROUP
N_NORM_HEADS = 2 * N_GROUPS * HEADS_PER_GROUP
N_HEAD_TILES = (COL_ZA + ATTN_OUT) // PROJ_TN
N_NORM_TILES = N_NORM_HEADS // HEADS_PER_TILE
REST_COL0 = N_HEAD_TILES * PROJ_TN
REST_COLS = IN_COLS - REST_COL0
assert N_NORM_HEADS % HEADS_PER_TILE == 0 and SEQ % PROJ_TM == 0 and PROJ_TM % CLS == 0
assert COL_ZA == N_QKV_HEADS * HEAD_DIM and (COL_ZA + ATTN_OUT) % PROJ_TN == 0
SILU_TILES = (COL_ZC // PROJ_TN, COL_ZM // PROJ_TN)
MQ_TILE = COL_MQ // PROJ_TN
assert CONV_WIDTH == PROJ_TN and MEM_W == PROJ_TN and COL_ZC % PROJ_TN == 0 and COL_MQ % PROJ_TN == 0


def _in_proj_kernel(x_ref, g_ref, hg_ref, mg_ref, w_ref, qkv_ref, o_ref, h_ref, hp_ref, tmp_ref):
    j = pl.program_id(1)
    rows_per_cls = PROJ_TM // CLS

    @pl.when(j == 0)
    def _():
        def body(c, _):
            rows = pl.ds(pl.multiple_of(c * ROW_CHUNK, ROW_CHUNK), ROW_CHUNK)
            hf = _rms(x_ref[rows, :], g_ref[...])
            h_ref[rows, :] = hf.astype(BF16)
            piece = ROW_CHUNK // CLS
            for s in range(D_MODEL // HEAD_DIM):
                lanes = slice(s * HEAD_DIM, (s + 1) * HEAD_DIM)
                tmp_ref[s] = hf[:, lanes]
                for b in range(CLS):
                    dst = pl.ds(pl.multiple_of(b * rows_per_cls + c * piece, piece), piece)
                    hp_ref[dst, lanes] = tmp_ref[s, pl.ds(b, piece, stride=CLS), :].astype(BF16)
            return _
        lax.fori_loop(0, PROJ_TM // ROW_CHUNK, body, None)

    def emit_heads(normed):
        res = jnp.dot(hp_ref[...], w_ref[...], preferred_element_type=F32)
        gains = hg_ref[pl.ds(pl.multiple_of(j * HEADS_PER_TILE, HEADS_PER_TILE), HEADS_PER_TILE), :]
        for s in range(HEADS_PER_TILE):
            r = res[:, s * HEAD_DIM:(s + 1) * HEAD_DIM]
            if normed:
                r = _rms(r, gains[s:s + 1, :])
            for b in range(CLS):
                qkv_ref[s, b] = r[b * rows_per_cls:(b + 1) * rows_per_cls].astype(qkv_ref.dtype)

    @pl.when(j < N_NORM_TILES)
    def _():
        emit_heads(True)

    @pl.when((j >= N_NORM_TILES) & (j < N_HEAD_TILES))
    def _():
        emit_heads(False)

    def project():
        return jnp.dot(h_ref[...], w_ref[...], preferred_element_type=F32)

    is_silu = (j == SILU_TILES[0]) | (j == SILU_TILES[1])

    @pl.when(is_silu)
    def _():
        o_ref[...] = _silu(project()).astype(o_ref.dtype)

    @pl.when(j == MQ_TILE)
    def _():
        res = project()
        for h in range(MEM_HEADS):
            cols = slice(h * MEM_HEAD_DIM, (h + 1) * MEM_HEAD_DIM)
            o_ref[:, cols] = _rms(res[:, cols], mg_ref[...]).astype(o_ref.dtype)

    @pl.when((j >= N_HEAD_TILES) & jnp.logical_not(is_silu) & (j != MQ_TILE))
    def _():
        o_ref[...] = project().astype(o_ref.dtype)


def _in_proj(x2, g, head_gains, mem_q_gain, w):
    n = x2.shape[0]
    tiles_per_seq = SEQ // PROJ_TM
    rows = PROJ_TM // CLS
    return pl.pallas_call(
        _in_proj_kernel,
        grid=(n // PROJ_TM, IN_COLS // PROJ_TN),
        in_specs=[
            pl.BlockSpec((PROJ_TM, D_MODEL), lambda i, j: (i, 0)),
            pl.BlockSpec((1, D_MODEL), lambda i, j: (0, 0)),
            pl.BlockSpec((N_HEAD_TILES * HEADS_PER_TILE, HEAD_DIM), lambda i, j: (0, 0)),
            pl.BlockSpec((1, MEM_HEAD_DIM), lambda i, j: (0, 0)),
            pl.BlockSpec((D_MODEL, PROJ_TN), lambda i, j: (0, j)),
        ],
        out_specs=[
            pl.BlockSpec((HEADS_PER_TILE, None, CLS, rows, HEAD_DIM),
                         lambda i, j: (jnp.minimum(j, N_HEAD_TILES - 1), i // tiles_per_seq, 0,
                                       i % tiles_per_seq, 0)),
            pl.BlockSpec((PROJ_TM, PROJ_TN),
                         lambda i, j: (i, jnp.maximum(j, N_HEAD_TILES) - N_HEAD_TILES)),
        ],
        out_shape=[
            jax.ShapeDtypeStruct((N_HEAD_TILES * HEADS_PER_TILE, n // SEQ, CLS, CLS_LEN, HEAD_DIM), BF16),
            jax.ShapeDtypeStruct((n, REST_COLS), BF16),
        ],
        scratch_shapes=[pltpu.VMEM((PROJ_TM, D_MODEL), BF16), pltpu.VMEM((PROJ_TM, D_MODEL), BF16),
                        pltpu.VMEM((D_MODEL // HEAD_DIM, ROW_CHUNK, HEAD_DIM), F32)],
        compiler_params=pltpu.CompilerParams(
            dimension_semantics=("arbitrary", "arbitrary"),
            vmem_limit_bytes=VMEM_LIMIT_BYTES),
        name="in_proj",
    )(x2, g, head_gains, mem_q_gain, w)


def _mem_kv_kernel(mem_ref, g_ref, w_ref, kn_ref, mk_ref, mv_ref):
    mh = _rms(mem_ref[...], g_ref[...]).astype(BF16)
    kv = jnp.dot(mh, w_ref[...], preferred_element_type=F32)
    for h in range(MEM_HEADS):
        cols = slice(h * MEM_HEAD_DIM, (h + 1) * MEM_HEAD_DIM)
        mk_ref[:, cols] = _rms(kv[:, cols], kn_ref[...]).astype(BF16)
    mv_ref[...] = kv[:, MEM_W:].astype(BF16)


def _mem_kv(mem, g, w, kn):
    b = mem.shape[0]
    return pl.pallas_call(
        _mem_kv_kernel,
        grid=(b,),
        in_specs=[
            pl.BlockSpec((None, MEM_LEN, D_MODEL), lambda i: (i, 0, 0)),
            pl.BlockSpec((1, D_MODEL), lambda i: (0, 0)),
            pl.BlockSpec((D_MODEL, 2 * MEM_W), lambda i: (0, 0)),
            pl.BlockSpec((1, MEM_HEAD_DIM), lambda i: (0, 0)),
        ],
        out_specs=[
            pl.BlockSpec((None, MEM_LEN, MEM_W), lambda i: (i, 0, 0)),
            pl.BlockSpec((None, MEM_LEN, MEM_W), lambda i: (i, 0, 0)),
        ],
        out_shape=[jax.ShapeDtypeStruct((b, MEM_LEN, MEM_W), BF16)] * 2,
        compiler_params=pltpu.CompilerParams(
            dimension_semantics=("arbitrary",),
            vmem_limit_bytes=VMEM_LIMIT_BYTES),
        name="mem_kv",
    )(mem, g, w, kn)


def _band_biases(bias_ref):
    r = lax.broadcasted_iota(jnp.int32, (BLK, 2 * BLK), 0)
    c = lax.broadcasted_iota(jnp.int32, (BLK, 2 * BLK), 1)

    def put(idx, diff):
        in_band = lax.bitcast_convert_type(diff, jnp.uint32) <= jnp.uint32(BLK)
        bias_ref[idx] = jnp.where(in_band, 0.0, -jnp.inf)

    put(0, c - r)
    piece = BLK // CLS
    b, jq = r // piece, r % piece
    bk, jk = c // (2 * piece), c % (2 * piece)
    put(1, CLS * (piece + jq - jk) + (b - bk))
    bk, jk = c // piece, c % piece
    put(2, CLS * (jq - jk) + (b - bk))


def _load_rows(ref, h, pieces):
    parts = [ref[h, b, rows, :] for b, rows in pieces]
    return parts[0] if len(parts) == 1 else jnp.concatenate(parts, axis=0)


def _store_rows(ref, h, pieces, val):
    off = 0
    for b, rows in pieces:
        ref[h, b, rows, :] = val[off:off + rows.size]
        off += rows.size


def _attn_block(qp, kp, bias, first, q_ref, k_ref, v_ref, acc_ref, m_ref, l_ref):
    nk = bias.shape[1]
    ones = jnp.ones((nk, HEAD_DIM), BF16)
    heads_per_dot = 2 * BLK // nk
    zeros = jnp.zeros((nk, HEAD_DIM), BF16)
    for h0 in range(0, HEADS_PER_GROUP, heads_per_dot):
        hs = range(h0, h0 + heads_per_dot)
        qb = jnp.concatenate([_load_rows(q_ref, h, qp) for h in hs], axis=-1).astype(BF16)
        ks = [_load_rows(k_ref, h, kp).astype(BF16) for h in hs]
        vs = [_load_rows(v_ref, h, kp).astype(BF16) for h in hs]
        if heads_per_dot == 1:
            kb = ks[0]
            vb = jnp.concatenate([vs[0], ones], axis=-1)
            bias_w = bias
        else:
            kb = jnp.concatenate([jnp.concatenate([ks[0], zeros], axis=-1),
                                  jnp.concatenate([zeros, ks[1]], axis=-1)], axis=0)
            vb = jnp.concatenate([jnp.concatenate([vs[0], ones, zeros, zeros], axis=-1),
                                  jnp.concatenate([zeros, zeros, vs[1], ones], axis=-1)], axis=0)
            bias_w = jnp.concatenate([bias, bias], axis=-1)
        s = lax.dot_general(qb, kb, (((1,), (1,)), ((), ())), preferred_element_type=F32) + bias_w
        w = s.shape[1] // heads_per_dot
        m_olds, m_news = [], []
        for i, h in enumerate(hs):
            mb = jnp.max(s[:, i * w:(i + 1) * w], axis=-1, keepdims=True)
            if first:
                m_olds.append(None)
                m_news.append(jnp.broadcast_to(mb, (BLK, HEAD_DIM)))
            else:
                m_olds.append(_load_rows(m_ref, h, qp))
                m_news.append(jnp.maximum(m_olds[-1], mb))
        m_wide = jnp.concatenate([m for m in m_news for _ in range(w // HEAD_DIM)], axis=-1)
        p = jnp.exp2(s - m_wide).astype(BF16)
        pv = jnp.dot(p, vb, preferred_element_type=F32)
        for i, h in enumerate(hs):
            acc_new = pv[:, 2 * i * HEAD_DIM:(2 * i + 1) * HEAD_DIM]
            l_new = pv[:, (2 * i + 1) * HEAD_DIM:(2 * i + 2) * HEAD_DIM]
            if not first:
                alpha = jnp.exp2(m_olds[i] - m_news[i])
                acc_new = alpha * _load_rows(acc_ref, h, qp) + acc_new
                l_new = alpha * _load_rows(l_ref, h, qp) + l_new
            _store_rows(m_ref, h, qp, m_news[i])
            _store_rows(l_ref, h, qp, l_new)
            _store_rows(acc_ref, h, qp, acc_new)


def _attn_group(d, first, q_ref, k_ref, v_ref, bias_ref, acc_ref, m_ref, l_ref, stage_refs):
    if d == 16:
        def widen(c, _):
            rows = pl.ds(pl.multiple_of(c * ROW_CHUNK, ROW_CHUNK), ROW_CHUNK)
            for src_ref, dst_ref in zip((q_ref, k_ref, v_ref), stage_refs):
                for h in range(HEADS_PER_GROUP):
                    for b in range(CLS):
                        dst_ref[h, b, rows, :] = src_ref[h, b, rows, :].astype(F32)
            return _
        lax.fori_loop(0, CLS_LEN // ROW_CHUNK, widen, None)
        q_ref, k_ref, v_ref = stage_refs
    block = functools.partial(_attn_block, first=first, q_ref=q_ref, k_ref=k_ref, v_ref=v_ref,
                              acc_ref=acc_ref, m_ref=m_ref, l_ref=l_ref)
    if d == 16:
        for r in range(d):
            a, b = divmod(r, CLS)
            rows = [(b, pl.ds(a, BLK, stride=CLS))]
            block(rows, rows, bias_ref[0, :, BLK:])
    elif d == 4:
        for b in range(CLS):
            for bi in range(CLS_LEN // BLK):
                q0 = bi * BLK
                if bi == 0:
                    block([(b, pl.ds(q0, BLK))], [(b, pl.ds(q0, BLK))], bias_ref[0, :, BLK:])
                else:
                    block([(b, pl.ds(q0, BLK))], [(b, pl.ds(q0 - BLK, 2 * BLK))], bias_ref[0])
    else:
        piece = BLK // CLS
        for bi in range(SEQ // BLK):
            qp = [(b, pl.ds(bi * piece, piece)) for b in range(CLS)]
            if bi == 0:
                block(qp, qp, bias_ref[2, :, :BLK])
            else:
                kp = [(b, pl.ds((bi - 1) * piece, 2 * piece)) for b in range(CLS)]
                block(qp, kp, bias_ref[1])


def _attn_kernel(q_ref, k_ref, v_ref, z_ref, o_ref, bias_ref, acc_ref, m_ref, l_ref, nat_ref,
                 sq_ref, sk_ref, sv_ref):
    step = pl.program_id(1)
    _band_biases(bias_ref)

    for s in range(N_GROUPS):
        @pl.when(step == s)
        def _(s=s):
            _attn_group(ATTN_PATTERNS[N_GROUPS - 1 - s][1], s == 0, q_ref, k_ref, v_ref,
                        bias_ref, acc_ref, m_ref, l_ref, (sq_ref, sk_ref, sv_ref))

    @pl.when(step == N_GROUPS - 1)
    def _():
        half = CLS_LEN // 2
        for h in range(HEADS_PER_GROUP):
            for b in range(CLS):
                for k in range(2):
                    src = pl.ds(k * half, half)
                    nat_ref[h, pl.ds(b + k * half * CLS, half, stride=CLS), :] = (
                        acc_ref[h, b, src, :] / l_ref[h, b, src, :]
                        * _silu(z_ref[h, b, src, :].astype(F32)))

        def fin(c, _):
            rows = pl.ds(pl.multiple_of(c * ROW_CHUNK, ROW_CHUNK), ROW_CHUNK)
            for h in range(HEADS_PER_GROUP):
                o_ref[rows, h * HEAD_DIM:(h + 1) * HEAD_DIM] = nat_ref[h, rows, :].astype(o_ref.dtype)
            return _
        lax.fori_loop(0, SEQ // ROW_CHUNK, fin, None)


def _attn(qkv):
    nb = qkv.shape[1]
    n = nb * SEQ

    def head_spec(kind):
        return pl.BlockSpec((HEADS_PER_GROUP, None, CLS, CLS_LEN, HEAD_DIM),
                            lambda i, s: (kind * N_GROUPS + N_GROUPS - 1 - s, i, 0, 0, 0))

    slab = pltpu.VMEM((HEADS_PER_GROUP, CLS, CLS_LEN, HEAD_DIM), F32)
    return pl.pallas_call(
        _attn_kernel,
        grid=(nb, N_GROUPS),
        in_specs=[
            head_spec(0), head_spec(1), head_spec(2),
            pl.BlockSpec((HEADS_PER_GROUP, None, CLS, CLS_LEN, HEAD_DIM),
                         lambda i, s: (N_QKV_HEADS // HEADS_PER_GROUP, i, 0, 0, 0)),
        ],
        out_specs=pl.BlockSpec((SEQ, ATTN_OUT), lambda i, s: (i, 0)),
        out_shape=jax.ShapeDtypeStruct((n, ATTN_OUT), BF16),
        scratch_shapes=[pltpu.VMEM((3, BLK, 2 * BLK), F32), slab, slab, slab,
                        pltpu.VMEM((HEADS_PER_GROUP, SEQ, HEAD_DIM), F32), slab, slab, slab],
        compiler_params=pltpu.CompilerParams(
            dimension_semantics=("arbitrary", "arbitrary"),
            vmem_limit_bytes=VMEM_LIMIT_BYTES),
        name="attn",
    )(qkv, qkv, qkv, qkv)


def _merge_kernel(a_ref, cb_ref, cc_ref, cv_ref, zc_ref, hc_ref, hv_ref, mq_ref, zm_ref,
                  g0a_ref, g0b_ref, g1a_ref, g1b_ref, g2a_ref, g2b_ref,
                  x_ref, mk_ref, mv_ref, cw_ref,
                  wa_ref, wc_ref, wm_ref, wo_ref, o_ref, u_ref, mg_ref, mgp_ref, *, n_tiles):
    ts = MERGE_TS
    tiles_per_seq = SEQ // ts
    t = pl.program_id(0)
    seq_start = (jnp.minimum(t, n_tiles - 1) % tiles_per_seq) == 0

    @pl.when(t == 0)
    def _():
        mg_ref[...] = jnp.zeros_like(mg_ref)

    mgp_ref[...] = mg_ref[...]
    o_ref[...] = x_ref[...] + jnp.dot(mgp_ref[...], wo_ref[...], preferred_element_type=F32)

    u = cc_ref[...].astype(F32) * cv_ref[...].astype(F32)
    uh = hc_ref[...].astype(F32) * hv_ref[...].astype(F32)
    u_ref[0:HALO, :] = jnp.where(seq_start, 0.0, uh)
    u_ref[HALO:HALO + ts, :] = u
    y = (cw_ref[0:1, :] * u
         + cw_ref[1:2, :] * u_ref[HALO - 1:HALO - 1 + ts, :]
         + cw_ref[2:3, :] * u_ref[HALO - 2:HALO - 2 + ts, :])
    c = (cb_ref[...].astype(F32) * y * zc_ref[...].astype(F32)).astype(BF16)

    mo_parts = []
    for h in range(MEM_HEADS):
        cols = slice(h * MEM_HEAD_DIM, (h + 1) * MEM_HEAD_DIM)
        s = lax.dot_general(mq_ref[:, cols], mk_ref[:, cols], (((1,), (1,)), ((), ())),
                            preferred_element_type=F32)
        p = jnp.exp2(s - jnp.max(s, axis=-1, keepdims=True))
        den = jnp.sum(p, axis=-1, keepdims=True)
        oh = jnp.dot(p.astype(BF16), mv_ref[:, cols], preferred_element_type=F32) / den
        mo_parts.append((oh * zm_ref[:, cols].astype(F32)).astype(BF16))
    mo = jnp.concatenate(mo_parts, axis=-1)

    ya = jnp.dot(a_ref[...], wa_ref[...], preferred_element_type=F32)
    yc = jnp.dot(c, wc_ref[...], preferred_element_type=F32)
    ym = jnp.dot(mo, wm_ref[...], preferred_element_type=F32)

    half = D_MODEL // 2
    for part, (ga, gc, gm) in enumerate(((g0a_ref, g1a_ref, g2a_ref), (g0b_ref, g1b_ref, g2b_ref))):
        cols = slice(part * half, (part + 1) * half)
        mg_ref[:, cols] = (jax.nn.sigmoid(ga[...].astype(F32)) * ya[:, cols]
                           + jax.nn.sigmoid(gc[...].astype(F32)) * yc[:, cols]
                           + jax.nn.sigmoid(gm[...].astype(F32)) * ym[:, cols]).astype(BF16)


def _merge(a, rest, x2, mk, mv, conv_w, wa, wc, wm, wo):
    n = x2.shape[0]
    ts = MERGE_TS
    n_tiles = n // ts
    tiles_per_seq = SEQ // ts
    half = D_MODEL // 2

    def cur(t):
        return jnp.minimum(t, n_tiles - 1)

    def prev(t):
        return jnp.maximum(t - 1, 0)

    def col_spec(col, width):
        return pl.BlockSpec((ts, width), lambda t: (cur(t), (col - REST_COL0) // width))

    def halo_spec(col):
        return pl.BlockSpec((HALO, CONV_WIDTH),
                            lambda t: (jnp.maximum(cur(t) * (ts // HALO) - 1, 0),
                                       (col - REST_COL0) // CONV_WIDTH))

    def const_spec(shape):
        return pl.BlockSpec(shape, lambda t: (0,) * len(shape), pipeline_mode=pl.Buffered(1))

    mem_spec = pl.BlockSpec((None, MEM_LEN, MEM_W), lambda t: (cur(t) // tiles_per_seq, 0, 0))
    in_specs = [
        pl.BlockSpec((ts, ATTN_OUT), lambda t: (cur(t), 0)),
        col_spec(COL_CB, CONV_WIDTH), col_spec(COL_CC, CONV_WIDTH),
        col_spec(COL_CV, CONV_WIDTH), col_spec(COL_ZC, CONV_WIDTH),
        halo_spec(COL_CC), halo_spec(COL_CV),
        col_spec(COL_MQ, MEM_W), col_spec(COL_ZM, MEM_W),
    ] + [col_spec(COL_G + k * half, half) for k in range(2 * N_BRANCH)] + [
        pl.BlockSpec((ts, D_MODEL), lambda t: (prev(t), 0)),
        mem_spec, mem_spec,
        const_spec((CONV_K, CONV_WIDTH)),
        const_spec((ATTN_OUT, D_MODEL)), const_spec((CONV_WIDTH, D_MODEL)),
        const_spec((MEM_W, D_MODEL)), const_spec((D_MODEL, D_MODEL)),
    ]
    n_rest = 8 + 2 * N_BRANCH
    return pl.pallas_call(
        functools.partial(_merge_kernel, n_tiles=n_tiles),
        grid=(n_tiles + 1,),
        in_specs=in_specs,
        out_specs=pl.BlockSpec((ts, D_MODEL), lambda t: (prev(t), 0)),
        out_shape=jax.ShapeDtypeStruct((n, D_MODEL), F32),
        scratch_shapes=[pltpu.VMEM((HALO + ts, CONV_WIDTH), F32),
                        pltpu.VMEM((ts, D_MODEL), BF16), pltpu.VMEM((ts, D_MODEL), BF16)],
        compiler_params=pltpu.CompilerParams(
            dimension_semantics=("arbitrary",),
            vmem_limit_bytes=VMEM_LIMIT_BYTES),
        name="merge",
    )(a, *([rest] * n_rest), x2, mk, mv, conv_w, wa, wc, wm, wo)


def kernel(x, mem, norm_g, mem_norm_g, w_in, attn_q_norm, attn_k_norm, conv_w, mem_w_kv,
           mem_q_norm, mem_k_norm, w_br_attn, w_br_conv, w_br_mem, w_out):
    b, s, d = x.shape
    assert (s, d) == (SEQ, D_MODEL) and w_in.shape == (D_MODEL, IN_COLS)
    x2 = x.reshape(b * s, d)
    scale = HEAD_DIM ** -0.5 * LOG2E
    head_gains = jnp.concatenate([jnp.repeat(attn_q_norm * scale, HEADS_PER_GROUP, axis=0),
                                  jnp.repeat(attn_k_norm, HEADS_PER_GROUP, axis=0),
                                  jnp.ones(((N_HEAD_TILES - N_NORM_TILES) * HEADS_PER_TILE, HEAD_DIM), F32)],
                                 axis=0)
    mem_q_gain = mem_q_norm.reshape(1, MEM_HEAD_DIM) * (MEM_HEAD_DIM ** -0.5 * LOG2E)
    qkv, rest = _in_proj(x2, norm_g.reshape(1, d), head_gains, mem_q_gain, w_in.astype(BF16))
    mk, mv = _mem_kv(mem, mem_norm_g.reshape(1, d), mem_w_kv.astype(BF16),
                     mem_k_norm.reshape(1, MEM_HEAD_DIM))
    a = _attn(qkv)
    out = _merge(a, rest, x2, mk, mv, conv_w,
                 w_br_attn.astype(BF16), w_br_conv.astype(BF16), w_br_mem.astype(BF16),
                 w_out.astype(BF16))
    return out.reshape(b, s, d)
```

```python
import functools

import jax
import jax.numpy as jnp
from jax import lax
from jax.experimental import pallas as pl
from jax.experimental.pallas import tpu as pltpu

D_MODEL = 2048
SEQ = 2048
HEAD_DIM = 128
ATTN_PATTERNS = ((128, 1), (512, 4), (2048, 16))
N_GROUPS = len(ATTN_PATTERNS)
HEADS_PER_GROUP = 4
GROUP_W = HEADS_PER_GROUP * HEAD_DIM
ATTN_QKV = N_GROUPS * GROUP_W
ATTN_OUT = GROUP_W
BLK = 128
CONV_WIDTH = 1024
CONV_K = 3
MEM_LEN = 256
MEM_HEADS = 4
MEM_HEAD_DIM = 256
MEM_W = MEM_HEADS * MEM_HEAD_DIM
N_BRANCH = 3
EPS = 1e-6
LOG2E = 1.4426950408889634
IN_COLS = 3 * ATTN_QKV + ATTN_OUT + 4 * CONV_WIDTH + 2 * MEM_W + N_BRANCH * D_MODEL

COL_Q = 0
COL_K = ATTN_QKV
COL_V = 2 * ATTN_QKV
COL_ZA = 3 * ATTN_QKV
COL_CB = COL_ZA + ATTN_OUT
COL_CC = COL_CB + CONV_WIDTH
COL_CV = COL_CC + CONV_WIDTH
COL_ZC = COL_CV + CONV_WIDTH
COL_MQ = COL_ZC + CONV_WIDTH
COL_ZM = COL_MQ + MEM_W
COL_G = COL_ZM + MEM_W

F32 = jnp.float32
BF16 = jnp.bfloat16

VMEM_LIMIT_BYTES = 56 * 1024 * 1024

PROJ_TM = 1024
PROJ_TN = 1024
MERGE_TS = 256
HALO = 16
ROW_CHUNK = 256


def _rms(t, gain):
    return t * lax.rsqrt(jnp.mean(t * t, axis=-1, keepdims=True) + EPS) * gain


def _silu(z):
    return z * jax.nn.sigmoid(z)


CLS = 4
CLS_LEN = SEQ // CLS
HEADS_PER_TILE = PROJ_TN // HEAD_DIM
N_QKV_HEADS = 3 * N_GROUPS * HEADS_PER_GROUP
N_NORM_HEADS = 2 * N_GROUPS * HEADS_PER_GROUP
N_HEAD_TILES = (COL_ZA + ATTN_OUT) // PROJ_TN
N_NORM_TILES = N_NORM_HEADS // HEADS_PER_TILE
REST_COL0 = N_HEAD_TILES * PROJ_TN
REST_COLS = IN_COLS - REST_COL0
assert N_NORM_HEADS % HEADS_PER_TILE == 0 and SEQ % PROJ_TM == 0 and PROJ_TM % CLS == 0
assert COL_ZA == N_QKV_HEADS * HEAD_DIM and (COL_ZA + ATTN_OUT) % PROJ_TN == 0
SILU_TILES = (COL_ZC // PROJ_TN, COL_ZM // PROJ_TN)
MQ_TILE = COL_MQ // PROJ_TN
assert CONV_WIDTH == PROJ_TN and MEM_W == PROJ_TN and COL_ZC % PROJ_TN == 0 and COL_MQ % PROJ_TN == 0


def _in_proj_kernel(x_ref, g_ref, hg_ref, mg_ref, w_ref, qkv_ref, o_ref, h_ref, hp_ref, tmp_ref):
    j = pl.program_id(1)
    rows_per_cls = PROJ_TM // CLS

    @pl.when(j == 0)
    def _():
        def body(c, _):
            rows = pl.ds(pl.multiple_of(c * ROW_CHUNK, ROW_CHUNK), ROW_CHUNK)
            hf = _rms(x_ref[rows, :], g_ref[...])
            h_ref[rows, :] = hf.astype(BF16)
            piece = ROW_CHUNK // CLS
            for s in range(D_MODEL // HEAD_DIM):
                lanes = slice(s * HEAD_DIM, (s + 1) * HEAD_DIM)
                tmp_ref[s] = hf[:, lanes]
                for b in range(CLS):
                    dst = pl.ds(pl.multiple_of(b * rows_per_cls + c * piece, piece), piece)
                    hp_ref[dst, lanes] = tmp_ref[s, pl.ds(b, piece, stride=CLS), :].astype(BF16)
            return _
        lax.fori_loop(0, PROJ_TM // ROW_CHUNK, body, None)

    def emit_heads(normed):
        res = jnp.dot(hp_ref[...], w_ref[...], preferred_element_type=F32)
        gains = hg_ref[pl.ds(pl.multiple_of(j * HEADS_PER_TILE, HEADS_PER_TILE), HEADS_PER_TILE), :]
        for s in range(HEADS_PER_TILE):
            r = res[:, s * HEAD_DIM:(s + 1) * HEAD_DIM]
            if normed:
                r = _rms(r, gains[s:s + 1, :])
            for b in range(CLS):
                qkv_ref[s, b] = r[b * rows_per_cls:(b + 1) * rows_per_cls].astype(qkv_ref.dtype)

    @pl.when(j < N_NORM_TILES)
    def _():
        emit_heads(True)

    @pl.when((j >= N_NORM_TILES) & (j < N_HEAD_TILES))
    def _():
        emit_heads(False)

    def project():
        return jnp.dot(h_ref[...], w_ref[...], preferred_element_type=F32)

    is_silu = (j == SILU_TILES[0]) | (j == SILU_TILES[1])

    @pl.when(is_silu)
    def _():
        o_ref[...] = _silu(project()).astype(o_ref.dtype)

    @pl.when(j == MQ_TILE)
    def _():
        res = project()
        for h in range(MEM_HEADS):
            cols = slice(h * MEM_HEAD_DIM, (h + 1) * MEM_HEAD_DIM)
            o_ref[:, cols] = _rms(res[:, cols], mg_ref[...]).astype(o_ref.dtype)

    @pl.when((j >= N_HEAD_TILES) & jnp.logical_not(is_silu) & (j != MQ_TILE))
    def _():
        o_ref[...] = project().astype(o_ref.dtype)


def _in_proj(x2, g, head_gains, mem_q_gain, w):
    n = x2.shape[0]
    tiles_per_seq = SEQ // PROJ_TM
    rows = PROJ_TM // CLS
    return pl.pallas_call(
        _in_proj_kernel,
        grid=(n // PROJ_TM, IN_COLS // PROJ_TN),
        in_specs=[
            pl.BlockSpec((PROJ_TM, D_MODEL), lambda i, j: (i, 0)),
            pl.BlockSpec((1, D_MODEL), lambda i, j: (0, 0)),
            pl.BlockSpec((N_HEAD_TILES * HEADS_PER_TILE, HEAD_DIM), lambda i, j: (0, 0)),
            pl.BlockSpec((1, MEM_HEAD_DIM), lambda i, j: (0, 0)),
            pl.BlockSpec((D_MODEL, PROJ_TN), lambda i, j: (0, j)),
        ],
        out_specs=[
            pl.BlockSpec((HEADS_PER_TILE, None, CLS, rows, HEAD_DIM),
                         lambda i, j: (jnp.minimum(j, N_HEAD_TILES - 1), i // tiles_per_seq, 0,
                                       i % tiles_per_seq, 0)),
            pl.BlockSpec((PROJ_TM, PROJ_TN),
                         lambda i, j: (i, jnp.maximum(j, N_HEAD_TILES) - N_HEAD_TILES)),
        ],
        out_shape=[
            jax.ShapeDtypeStruct((N_HEAD_TILES * HEADS_PER_TILE, n // SEQ, CLS, CLS_LEN, HEAD_DIM), BF16),
            jax.ShapeDtypeStruct((n, REST_COLS), BF16),
        ],
        scratch_shapes=[pltpu.VMEM((PROJ_TM, D_MODEL), BF16), pltpu.VMEM((PROJ_TM, D_MODEL), BF16),
                        pltpu.VMEM((D_MODEL // HEAD_DIM, ROW_CHUNK, HEAD_DIM), F32)],
        compiler_params=pltpu.CompilerParams(
            dimension_semantics=("arbitrary", "arbitrary"),
            vmem_limit_bytes=VMEM_LIMIT_BYTES),
        name="in_proj",
    )(x2, g, head_gains, mem_q_gain, w)


MEM_ROWS = 512


def _mem_branch_kernel(mem_ref, g_ref, w_ref, kn_ref, mq_ref, zm_ref, mo_ref, mk_ref, mv_ref):
    mh = _rms(mem_ref[...], g_ref[...]).astype(BF16)
    kv = jnp.dot(mh, w_ref[...], preferred_element_type=F32)
    for h in range(MEM_HEADS):
        cols = slice(h * MEM_HEAD_DIM, (h + 1) * MEM_HEAD_DIM)
        mk_ref[:, cols] = _rms(kv[:, cols], kn_ref[...]).astype(BF16)
    mv_ref[...] = kv[:, MEM_W:].astype(BF16)

    for c in range(SEQ // MEM_ROWS):
        rows = pl.ds(c * MEM_ROWS, MEM_ROWS)
        for h in range(MEM_HEADS):
            cols = slice(h * MEM_HEAD_DIM, (h + 1) * MEM_HEAD_DIM)
            s = lax.dot_general(mq_ref[rows, cols], mk_ref[:, cols], (((1,), (1,)), ((), ())),
                                preferred_element_type=F32)
            p = jnp.exp2(s - jnp.max(s, axis=-1, keepdims=True))
            den = jnp.sum(p, axis=-1, keepdims=True)
            oh = jnp.dot(p.astype(BF16), mv_ref[:, cols], preferred_element_type=F32) / den
            mo_ref[rows, cols] = (oh * zm_ref[rows, cols].astype(F32)).astype(BF16)


def _mem_branch(mem, g, w, kn, rest):
    b = mem.shape[0]

    def tok_spec(col):
        return pl.BlockSpec((SEQ, MEM_W), lambda i: (i, (col - REST_COL0) // MEM_W))

    return pl.pallas_call(
        _mem_branch_kernel,
        grid=(b,),
        in_specs=[
            pl.BlockSpec((None, MEM_LEN, D_MODEL), lambda i: (i, 0, 0)),
            pl.BlockSpec((1, D_MODEL), lambda i: (0, 0)),
            pl.BlockSpec((D_MODEL, 2 * MEM_W), lambda i: (0, 0), pipeline_mode=pl.Buffered(1)),
            pl.BlockSpec((1, MEM_HEAD_DIM), lambda i: (0, 0)),
            tok_spec(COL_MQ), tok_spec(COL_ZM),
        ],
        out_specs=pl.BlockSpec((SEQ, MEM_W), lambda i: (i, 0)),
        out_shape=jax.ShapeDtypeStruct((b * SEQ, MEM_W), BF16),
        scratch_shapes=[pltpu.VMEM((MEM_LEN, MEM_W), BF16), pltpu.VMEM((MEM_LEN, MEM_W), BF16)],
        compiler_params=pltpu.CompilerParams(
            dimension_semantics=("arbitrary",),
            vmem_limit_bytes=VMEM_LIMIT_BYTES),
        name="mem_branch",
    )(mem, g, w, kn, rest, rest)


def _band_biases(bias_ref):
    r = lax.broadcasted_iota(jnp.int32, (BLK, 2 * BLK), 0)
    c = lax.broadcasted_iota(jnp.int32, (BLK, 2 * BLK), 1)

    def put(idx, diff):
        in_band = lax.bitcast_convert_type(diff, jnp.uint32) <= jnp.uint32(BLK)
        bias_ref[idx] = jnp.where(in_band, 0.0, -jnp.inf)

    put(0, c - r)
    piece = BLK // CLS
    b, jq = r // piece, r % piece
    bk, jk = c // (2 * piece), c % (2 * piece)
    put(1, CLS * (piece + jq - jk) + (b - bk))
    bk, jk = c // piece, c % piece
    put(2, CLS * (jq - jk) + (b - bk))


def _load_rows(ref, h, pieces):
    parts = [ref[h, b, rows, :] for b, rows in pieces]
    return parts[0] if len(parts) == 1 else jnp.concatenate(parts, axis=0)


def _store_rows(ref, h, pieces, val):
    off = 0
    for b, rows in pieces:
        ref[h, b, rows, :] = val[off:off + rows.size]
        off += rows.size


def _attn_block(qp, kp, bias, first, q_ref, k_ref, v_ref, acc_ref, m_ref, l_ref):
    nk = bias.shape[1]
    ones = jnp.ones((nk, HEAD_DIM), BF16)
    heads_per_dot = 2 * BLK // nk
    zeros = jnp.zeros((nk, HEAD_DIM), BF16)
    for h0 in range(0, HEADS_PER_GROUP, heads_per_dot):
        hs = range(h0, h0 + heads_per_dot)
        qb = jnp.concatenate([_load_rows(q_ref, h, qp) for h in hs], axis=-1).astype(BF16)
        ks = [_load_rows(k_ref, h, kp).astype(BF16) for h in hs]
        vs = [_load_rows(v_ref, h, kp).astype(BF16) for h in hs]
        if heads_per_dot == 1:
            kb = ks[0]
            vb = jnp.concatenate([vs[0], ones], axis=-1)
            bias_w = bias
        else:
            kb = jnp.concatenate([jnp.concatenate([ks[0], zeros], axis=-1),
                                  jnp.concatenate([zeros, ks[1]], axis=-1)], axis=0)
            vb = jnp.concatenate([jnp.concatenate([vs[0], ones, zeros, zeros], axis=-1),
                                  jnp.concatenate([zeros, zeros, vs[1], ones], axis=-1)], axis=0)
            bias_w = jnp.concatenate([bias, bias], axis=-1)
        s = lax.dot_general(qb, kb, (((1,), (1,)), ((), ())), preferred_element_type=F32) + bias_w
        w = s.shape[1] // heads_per_dot
        m_olds, m_news = [], []
        for i, h in enumerate(hs):
            mb = jnp.max(s[:, i * w:(i + 1) * w], axis=-1, keepdims=True)
            if first:
                m_olds.append(None)
                m_news.append(jnp.broadcast_to(mb, (BLK, HEAD_DIM)))
            else:
                m_olds.append(_load_rows(m_ref, h, qp))
                m_news.append(jnp.maximum(m_olds[-1], mb))
        m_wide = jnp.concatenate([m for m in m_news for _ in range(w // HEAD_DIM)], axis=-1)
        p = jnp.exp2(s - m_wide).astype(BF16)
        pv = jnp.dot(p, vb, preferred_element_type=F32)
        for i, h in enumerate(hs):
            acc_new = pv[:, 2 * i * HEAD_DIM:(2 * i + 1) * HEAD_DIM]
            l_new = pv[:, (2 * i + 1) * HEAD_DIM:(2 * i + 2) * HEAD_DIM]
            if not first:
                alpha = jnp.exp2(m_olds[i] - m_news[i])
                acc_new = alpha * _load_rows(acc_ref, h, qp) + acc_new
                l_new = alpha * _load_rows(l_ref, h, qp) + l_new
            _store_rows(m_ref, h, qp, m_news[i])
            _store_rows(l_ref, h, qp, l_new)
            _store_rows(acc_ref, h, qp, acc_new)


def _attn_group(d, first, q_ref, k_ref, v_ref, bias_ref, acc_ref, m_ref, l_ref, stage_refs):
    if d == 16:
        def widen(c, _):
            rows = pl.ds(pl.multiple_of(c * ROW_CHUNK, ROW_CHUNK), ROW_CHUNK)
            for src_ref, dst_ref in zip((q_ref, k_ref, v_ref), stage_refs):
                for h in range(HEADS_PER_GROUP):
                    for b in range(CLS):
                        dst_ref[h, b, rows, :] = src_ref[h, b, rows, :].astype(F32)
            return _
        lax.fori_loop(0, CLS_LEN // ROW_CHUNK, widen, None)
        q_ref, k_ref, v_ref = stage_refs
    block = functools.partial(_attn_block, first=first, q_ref=q_ref, k_ref=k_ref, v_ref=v_ref,
                              acc_ref=acc_ref, m_ref=m_ref, l_ref=l_ref)
    if d == 16:
        for r in range(d):
            a, b = divmod(r, CLS)
            rows = [(b, pl.ds(a, BLK, stride=CLS))]
            block(rows, rows, bias_ref[0, :, BLK:])
    elif d == 4:
        for b in range(CLS):
            for bi in range(CLS_LEN // BLK):
                q0 = bi * BLK
                if bi == 0:
                    block([(b, pl.ds(q0, BLK))], [(b, pl.ds(q0, BLK))], bias_ref[0, :, BLK:])
                else:
                    block([(b, pl.ds(q0, BLK))], [(b, pl.ds(q0 - BLK, 2 * BLK))], bias_ref[0])
    else:
        piece = BLK // CLS
        for bi in range(SEQ // BLK):
            qp = [(b, pl.ds(bi * piece, piece)) for b in range(CLS)]
            if bi == 0:
                block(qp, qp, bias_ref[2, :, :BLK])
            else:
                kp = [(b, pl.ds((bi - 1) * piece, 2 * piece)) for b in range(CLS)]
                block(qp, kp, bias_ref[1])


def _attn_kernel(q_ref, k_ref, v_ref, z_ref, o_ref, bias_ref, acc_ref, m_ref, l_ref, nat_ref,
                 sq_ref, sk_ref, sv_ref):
    step = pl.program_id(1)
    _band_biases(bias_ref)

    for s in range(N_GROUPS):
        @pl.when(step == s)
        def _(s=s):
            _attn_group(ATTN_PATTERNS[N_GROUPS - 1 - s][1], s == 0, q_ref, k_ref, v_ref,
                        bias_ref, acc_ref, m_ref, l_ref, (sq_ref, sk_ref, sv_ref))

    @pl.when(step == N_GROUPS - 1)
    def _():
        half = CLS_LEN // 2
        for h in range(HEADS_PER_GROUP):
            for b in range(CLS):
                for k in range(2):
                    src = pl.ds(k * half, half)
                    nat_ref[h, pl.ds(b + k * half * CLS, half, stride=CLS), :] = (
                        acc_ref[h, b, src, :] / l_ref[h, b, src, :]
                        * _silu(z_ref[h, b, src, :].astype(F32)))

        def fin(c, _):
            rows = pl.ds(pl.multiple_of(c * ROW_CHUNK, ROW_CHUNK), ROW_CHUNK)
            for h in range(HEADS_PER_GROUP):
                o_ref[rows, h * HEAD_DIM:(h + 1) * HEAD_DIM] = nat_ref[h, rows, :].astype(o_ref.dtype)
            return _
        lax.fori_loop(0, SEQ // ROW_CHUNK, fin, None)


def _attn(qkv):
    nb = qkv.shape[1]
    n = nb * SEQ

    def head_spec(kind):
        return pl.BlockSpec((HEADS_PER_GROUP, None, CLS, CLS_LEN, HEAD_DIM),
                            lambda i, s: (kind * N_GROUPS + N_GROUPS - 1 - s, i, 0, 0, 0))

    slab = pltpu.VMEM((HEADS_PER_GROUP, CLS, CLS_LEN, HEAD_DIM), F32)
    return pl.pallas_call(
        _attn_kernel,
        grid=(nb, N_GROUPS),
        in_specs=[
            head_spec(0), head_spec(1), head_spec(2),
            pl.BlockSpec((HEADS_PER_GROUP, None, CLS, CLS_LEN, HEAD_DIM),
                         lambda i, s: (N_QKV_HEADS // HEADS_PER_GROUP, i, 0, 0, 0)),
        ],
        out_specs=pl.BlockSpec((SEQ, ATTN_OUT), lambda i, s: (i, 0)),
        out_shape=jax.ShapeDtypeStruct((n, ATTN_OUT), BF16),
        scratch_shapes=[pltpu.VMEM((3, BLK, 2 * BLK), F32), slab, slab, slab,
                        pltpu.VMEM((HEADS_PER_GROUP, SEQ, HEAD_DIM), F32), slab, slab, slab],
        compiler_params=pltpu.CompilerParams(
            dimension_semantics=("arbitrary", "arbitrary"),
            vmem_limit_bytes=VMEM_LIMIT_BYTES),
        name="attn",
    )(qkv, qkv, qkv, qkv)


def _merge_kernel(a_ref, cb_ref, cc_ref, cv_ref, zc_ref, hc_ref, hv_ref, mo_ref,
                  g0a_ref, g0b_ref, g1a_ref, g1b_ref, g2a_ref, g2b_ref,
                  x_ref, cw_ref,
                  wa_ref, wc_ref, wm_ref, wo_ref, o_ref, u_ref, mg_ref, mgp_ref, *, n_tiles):
    ts = MERGE_TS
    tiles_per_seq = SEQ // ts
    t = pl.program_id(0)
    seq_start = (jnp.minimum(t, n_tiles - 1) % tiles_per_seq) == 0

    @pl.when(t == 0)
    def _():
        mg_ref[...] = jnp.zeros_like(mg_ref)

    mgp_ref[...] = mg_ref[...]
    o_ref[...] = x_ref[...] + jnp.dot(mgp_ref[...], wo_ref[...], preferred_element_type=F32)

    u = cc_ref[...].astype(F32) * cv_ref[...].astype(F32)
    uh = hc_ref[...].astype(F32) * hv_ref[...].astype(F32)
    u_ref[0:HALO, :] = jnp.where(seq_start, 0.0, uh)
    u_ref[HALO:HALO + ts, :] = u
    y = (cw_ref[0:1, :] * u
         + cw_ref[1:2, :] * u_ref[HALO - 1:HALO - 1 + ts, :]
         + cw_ref[2:3, :] * u_ref[HALO - 2:HALO - 2 + ts, :])
    c = (cb_ref[...].astype(F32) * y * zc_ref[...].astype(F32)).astype(BF16)

    ya = jnp.dot(a_ref[...], wa_ref[...], preferred_element_type=F32)
    yc = jnp.dot(c, wc_ref[...], preferred_element_type=F32)
    ym = jnp.dot(mo_ref[...], wm_ref[...], preferred_element_type=F32)

    half = D_MODEL // 2
    for part, (ga, gc, gm) in enumerate(((g0a_ref, g1a_ref, g2a_ref), (g0b_ref, g1b_ref, g2b_ref))):
        cols = slice(part * half, (part + 1) * half)
        mg_ref[:, cols] = (jax.nn.sigmoid(ga[...].astype(F32)) * ya[:, cols]
                           + jax.nn.sigmoid(gc[...].astype(F32)) * yc[:, cols]
                           + jax.nn.sigmoid(gm[...].astype(F32)) * ym[:, cols]).astype(BF16)


def _merge(a, rest, mo, x2, conv_w, wa, wc, wm, wo):
    n = x2.shape[0]
    ts = MERGE_TS
    n_tiles = n // ts
    half = D_MODEL // 2

    def cur(t):
        return jnp.minimum(t, n_tiles - 1)

    def prev(t):
        return jnp.maximum(t - 1, 0)

    def col_spec(col, width):
        return pl.BlockSpec((ts, width), lambda t: (cur(t), (col - REST_COL0) // width))

    def halo_spec(col):
        return pl.BlockSpec((HALO, CONV_WIDTH),
                            lambda t: (jnp.maximum(cur(t) * (ts // HALO) - 1, 0),
                                       (col - REST_COL0) // CONV_WIDTH))

    def const_spec(shape):
        return pl.BlockSpec(shape, lambda t: (0,) * len(shape), pipeline_mode=pl.Buffered(1))

    in_specs = [
        pl.BlockSpec((ts, ATTN_OUT), lambda t: (cur(t), 0)),
        col_spec(COL_CB, CONV_WIDTH), col_spec(COL_CC, CONV_WIDTH),
        col_spec(COL_CV, CONV_WIDTH), col_spec(COL_ZC, CONV_WIDTH),
        halo_spec(COL_CC), halo_spec(COL_CV),
        pl.BlockSpec((ts, MEM_W), lambda t: (cur(t), 0)),
    ] + [col_spec(COL_G + k * half, half) for k in range(2 * N_BRANCH)] + [
        pl.BlockSpec((ts, D_MODEL), lambda t: (prev(t), 0)),
        const_spec((CONV_K, CONV_WIDTH)),
        const_spec((ATTN_OUT, D_MODEL)), const_spec((CONV_WIDTH, D_MODEL)),
        const_spec((MEM_W, D_MODEL)), const_spec((D_MODEL, D_MODEL)),
    ]
    n_head, n_gate = 6, 2 * N_BRANCH
    return pl.pallas_call(
        functools.partial(_merge_kernel, n_tiles=n_tiles),
        grid=(n_tiles + 1,),
        in_specs=in_specs,
        out_specs=pl.BlockSpec((ts, D_MODEL), lambda t: (prev(t), 0)),
        out_shape=jax.ShapeDtypeStruct((n, D_MODEL), F32),
        scratch_shapes=[pltpu.VMEM((HALO + ts, CONV_WIDTH), F32),
                        pltpu.VMEM((ts, D_MODEL), BF16), pltpu.VMEM((ts, D_MODEL), BF16)],
        compiler_params=pltpu.CompilerParams(
            dimension_semantics=("arbitrary",),
            vmem_limit_bytes=VMEM_LIMIT_BYTES),
        name="merge",
    )(a, *([rest] * n_head), mo, *([rest] * n_gate), x2, conv_w, wa, wc, wm, wo)


def kernel(x, mem, norm_g, mem_norm_g, w_in, attn_q_norm, attn_k_norm, conv_w, mem_w_kv,
           mem_q_norm, mem_k_norm, w_br_attn, w_br_conv, w_br_mem, w_out):
    b, s, d = x.shape
    assert (s, d) == (SEQ, D_MODEL) and w_in.shape == (D_MODEL, IN_COLS)
    x2 = x.reshape(b * s, d)
    scale = HEAD_DIM ** -0.5 * LOG2E
    head_gains = jnp.concatenate([jnp.repeat(attn_q_norm * scale, HEADS_PER_GROUP, axis=0),
                                  jnp.repeat(attn_k_norm, HEADS_PER_GROUP, axis=0),
                                  jnp.ones(((N_HEAD_TILES - N_NORM_TILES) * HEADS_PER_TILE, HEAD_DIM), F32)],
                                 axis=0)
    mem_q_gain = mem_q_norm.reshape(1, MEM_HEAD_DIM) * (MEM_HEAD_DIM ** -0.5 * LOG2E)
    qkv, rest = _in_proj(x2, norm_g.reshape(1, d), head_gains, mem_q_gain, w_in.astype(BF16))
    mo = _mem_branch(mem, mem_norm_g.reshape(1, d), mem_w_kv.astype(BF16),
                     mem_k_norm.reshape(1, MEM_HEAD_DIM), rest)
    a = _attn(qkv)
    out = _merge(a, rest, mo, x2, conv_w,
                 w_br_attn.astype(BF16), w_br_conv.astype(BF16), w_br_mem.astype(BF16),
                 w_out.astype(BF16))
    return out.reshape(b, s, d)
```

```python
import functools

import jax
import jax.numpy as jnp
from jax import lax
from jax.experimental import pallas as pl
from jax.experimental.pallas import tpu as pltpu

D_MODEL = 2048
SEQ = 2048
HEAD_DIM = 128
ATTN_PATTERNS = ((128, 1), (512, 4), (2048, 16))
N_GROUPS = len(ATTN_PATTERNS)
HEADS_PER_GROUP = 4
GROUP_W = HEADS_PER_GROUP * HEAD_DIM
ATTN_QKV = N_GROUPS * GROUP_W
ATTN_OUT = GROUP_W
BLK = 128
CONV_WIDTH = 1024
CONV_K = 3
MEM_LEN = 256
MEM_HEADS = 4
MEM_HEAD_DIM = 256
MEM_W = MEM_HEADS * MEM_HEAD_DIM
N_BRANCH = 3
EPS = 1e-6
LOG2E = 1.4426950408889634
IN_COLS = 3 * ATTN_QKV + ATTN_OUT + 4 * CONV_WIDTH + 2 * MEM_W + N_BRANCH * D_MODEL

COL_Q = 0
COL_K = ATTN_QKV
COL_V = 2 * ATTN_QKV
COL_ZA = 3 * ATTN_QKV
COL_CB = COL_ZA + ATTN_OUT
COL_CC = COL_CB + CONV_WIDTH
COL_CV = COL_CC + CONV_WIDTH
COL_ZC = COL_CV + CONV_WIDTH
COL_MQ = COL_ZC + CONV_WIDTH
COL_ZM = COL_MQ + MEM_W
COL_G = COL_ZM + MEM_W

F32 = jnp.float32
BF16 = jnp.bfloat16

VMEM_LIMIT_BYTES = 56 * 1024 * 1024

PROJ_TM = 1024
PROJ_TN = 1024
MERGE_TS = 256
HALO = 16
ROW_CHUNK = 256


def _rms(t, gain):
    return t * lax.rsqrt(jnp.mean(t * t, axis=-1, keepdims=True) + EPS) * gain


def _silu(z):
    return z * jax.nn.sigmoid(z)


CLS = 4
CLS_LEN = SEQ // CLS
HEADS_PER_TILE = PROJ_TN // HEAD_DIM
N_QKV_HEADS = 3 * N_GROUPS * HEADS_PER_GROUP
N_NORM_HEADS = 2 * N_GROUPS * HEADS_PER_GROUP
N_HEAD_TILES = (COL_ZA + ATTN_OUT) // PROJ_TN
N_NORM_TILES = N_NORM_HEADS // HEADS_PER_TILE
REST_COL0 = N_HEAD_TILES * PROJ_TN
REST_COLS = IN_COLS - REST_COL0
assert N_NORM_HEADS % HEADS_PER_TILE == 0 and SEQ % PROJ_TM == 0 and PROJ_TM % CLS == 0
assert COL_ZA == N_QKV_HEADS * HEAD_DIM and (COL_ZA + ATTN_OUT) % PROJ_TN == 0
SILU_TILES = (COL_ZC // PROJ_TN, COL_ZM // PROJ_TN)
MQ_TILE = COL_MQ // PROJ_TN
assert CONV_WIDTH == PROJ_TN and MEM_W == PROJ_TN and COL_ZC % PROJ_TN == 0 and COL_MQ % PROJ_TN == 0


def _in_proj_kernel(x_ref, g_ref, hg_ref, mg_ref, w_ref, qkv_ref, o_ref, h_ref, hp_ref, tmp_ref):
    j = pl.program_id(1)
    rows_per_cls = PROJ_TM // CLS

    @pl.when(j == 0)
    def _():
        def body(c, _):
            rows = pl.ds(pl.multiple_of(c * ROW_CHUNK, ROW_CHUNK), ROW_CHUNK)
            hf = _rms(x_ref[rows, :], g_ref[...])
            h_ref[rows, :] = hf.astype(BF16)
            piece = ROW_CHUNK // CLS
            for s in range(D_MODEL // HEAD_DIM):
                lanes = slice(s * HEAD_DIM, (s + 1) * HEAD_DIM)
                tmp_ref[s] = hf[:, lanes]
                for b in range(CLS):
                    dst = pl.ds(pl.multiple_of(b * rows_per_cls + c * piece, piece), piece)
                    hp_ref[dst, lanes] = tmp_ref[s, pl.ds(b, piece, stride=CLS), :].astype(BF16)
            return _
        lax.fori_loop(0, PROJ_TM // ROW_CHUNK, body, None)

    def emit_heads(normed):
        res = jnp.dot(hp_ref[...], w_ref[...], preferred_element_type=F32)
        gains = hg_ref[pl.ds(pl.multiple_of(j * HEADS_PER_TILE, HEADS_PER_TILE), HEADS_PER_TILE), :]
        for s in range(HEADS_PER_TILE):
            r = res[:, s * HEAD_DIM:(s + 1) * HEAD_DIM]
            if normed:
                r = _rms(r, gains[s:s + 1, :])
            for b in range(CLS):
                qkv_ref[s, b] = r[b * rows_per_cls:(b + 1) * rows_per_cls].astype(qkv_ref.dtype)

    @pl.when(j < N_NORM_TILES)
    def _():
        emit_heads(True)

    @pl.when((j >= N_NORM_TILES) & (j < N_HEAD_TILES))
    def _():
        emit_heads(False)

    def project():
        return jnp.dot(h_ref[...], w_ref[...], preferred_element_type=F32)

    is_silu = (j == SILU_TILES[0]) | (j == SILU_TILES[1])

    @pl.when(is_silu)
    def _():
        o_ref[...] = _silu(project()).astype(o_ref.dtype)

    @pl.when(j == MQ_TILE)
    def _():
        res = project()
        for h in range(MEM_HEADS):
            cols = slice(h * MEM_HEAD_DIM, (h + 1) * MEM_HEAD_DIM)
            o_ref[:, cols] = _rms(res[:, cols], mg_ref[...]).astype(o_ref.dtype)

    @pl.when((j >= N_HEAD_TILES) & jnp.logical_not(is_silu) & (j != MQ_TILE))
    def _():
        o_ref[...] = project().astype(o_ref.dtype)


def _in_proj(x2, g, head_gains, mem_q_gain, w):
    n = x2.shape[0]
    tiles_per_seq = SEQ // PROJ_TM
    rows = PROJ_TM // CLS
    return pl.pallas_call(
        _in_proj_kernel,
        grid=(n // PROJ_TM, IN_COLS // PROJ_TN),
        in_specs=[
            pl.BlockSpec((PROJ_TM, D_MODEL), lambda i, j: (i, 0)),
            pl.BlockSpec((1, D_MODEL), lambda i, j: (0, 0)),
            pl.BlockSpec((N_HEAD_TILES * HEADS_PER_TILE, HEAD_DIM), lambda i, j: (0, 0)),
            pl.BlockSpec((1, MEM_HEAD_DIM), lambda i, j: (0, 0)),
            pl.BlockSpec((D_MODEL, PROJ_TN), lambda i, j: (0, j)),
        ],
        out_specs=[
            pl.BlockSpec((HEADS_PER_TILE, None, CLS, rows, HEAD_DIM),
                         lambda i, j: (jnp.minimum(j, N_HEAD_TILES - 1), i // tiles_per_seq, 0,
                                       i % tiles_per_seq, 0)),
            pl.BlockSpec((PROJ_TM, PROJ_TN),
                         lambda i, j: (i, jnp.maximum(j, N_HEAD_TILES) - N_HEAD_TILES)),
        ],
        out_shape=[
            jax.ShapeDtypeStruct((N_HEAD_TILES * HEADS_PER_TILE, n // SEQ, CLS, CLS_LEN, HEAD_DIM), BF16),
            jax.ShapeDtypeStruct((n, REST_COLS), BF16),
        ],
        scratch_shapes=[pltpu.VMEM((PROJ_TM, D_MODEL), BF16), pltpu.VMEM((PROJ_TM, D_MODEL), BF16),
                        pltpu.VMEM((D_MODEL // HEAD_DIM, ROW_CHUNK, HEAD_DIM), F32)],
        compiler_params=pltpu.CompilerParams(
            dimension_semantics=("arbitrary", "arbitrary"),
            vmem_limit_bytes=VMEM_LIMIT_BYTES),
        name="in_proj",
    )(x2, g, head_gains, mem_q_gain, w)


MEM_ROWS = 512


def _mem_branch_kernel(mem_ref, g_ref, w_ref, kn_ref, mq_ref, zm_ref, mo_ref, mk_ref, mv_ref):
    mh = _rms(mem_ref[...], g_ref[...]).astype(BF16)
    kv = jnp.dot(mh, w_ref[...], preferred_element_type=F32)
    for h in range(MEM_HEADS):
        cols = slice(h * MEM_HEAD_DIM, (h + 1) * MEM_HEAD_DIM)
        mk_ref[:, cols] = _rms(kv[:, cols], kn_ref[...]).astype(BF16)
    mv_ref[...] = kv[:, MEM_W:].astype(BF16)

    for c in range(SEQ // MEM_ROWS):
        rows = pl.ds(c * MEM_ROWS, MEM_ROWS)
        for h in range(MEM_HEADS):
            cols = slice(h * MEM_HEAD_DIM, (h + 1) * MEM_HEAD_DIM)
            s = lax.dot_general(mq_ref[rows, cols], mk_ref[:, cols], (((1,), (1,)), ((), ())),
                                preferred_element_type=F32)
            p = jnp.exp2(s - jnp.max(s, axis=-1, keepdims=True))
            den = jnp.sum(p, axis=-1, keepdims=True)
            oh = jnp.dot(p.astype(BF16), mv_ref[:, cols], preferred_element_type=F32) / den
            mo_ref[rows, cols] = (oh * zm_ref[rows, cols].astype(F32)).astype(BF16)


def _mem_branch(mem, g, w, kn, rest):
    b = mem.shape[0]

    def tok_spec(col):
        return pl.BlockSpec((SEQ, MEM_W), lambda i: (i, (col - REST_COL0) // MEM_W))

    return pl.pallas_call(
        _mem_branch_kernel,
        grid=(b,),
        in_specs=[
            pl.BlockSpec((None, MEM_LEN, D_MODEL), lambda i: (i, 0, 0)),
            pl.BlockSpec((1, D_MODEL), lambda i: (0, 0)),
            pl.BlockSpec((D_MODEL, 2 * MEM_W), lambda i: (0, 0), pipeline_mode=pl.Buffered(1)),
            pl.BlockSpec((1, MEM_HEAD_DIM), lambda i: (0, 0)),
            tok_spec(COL_MQ), tok_spec(COL_ZM),
        ],
        out_specs=pl.BlockSpec((SEQ, MEM_W), lambda i: (i, 0)),
        out_shape=jax.ShapeDtypeStruct((b * SEQ, MEM_W), BF16),
        scratch_shapes=[pltpu.VMEM((MEM_LEN, MEM_W), BF16), pltpu.VMEM((MEM_LEN, MEM_W), BF16)],
        compiler_params=pltpu.CompilerParams(
            dimension_semantics=("arbitrary",),
            vmem_limit_bytes=VMEM_LIMIT_BYTES),
        name="mem_branch",
    )(mem, g, w, kn, rest, rest)


def _band_biases(bias_ref):
    r = lax.broadcasted_iota(jnp.int32, (BLK, 2 * BLK), 0)
    c = lax.broadcasted_iota(jnp.int32, (BLK, 2 * BLK), 1)

    def put(idx, diff):
        in_band = lax.bitcast_convert_type(diff, jnp.uint32) <= jnp.uint32(BLK)
        bias_ref[idx] = jnp.where(in_band, 0.0, -jnp.inf)

    put(0, c - r)
    piece = BLK // CLS
    b, jq = r // piece, r % piece
    bk, jk = c // (2 * piece), c % (2 * piece)
    put(1, CLS * (piece + jq - jk) + (b - bk))
    bk, jk = c // piece, c % piece
    put(2, CLS * (jq - jk) + (b - bk))


def _load_rows(ref, h, pieces):
    parts = [ref[h, b, rows, :] for b, rows in pieces]
    return parts[0] if len(parts) == 1 else jnp.concatenate(parts, axis=0)


def _store_rows(ref, h, pieces, val):
    off = 0
    for b, rows in pieces:
        ref[h, b, rows, :] = val[off:off + rows.size]
        off += rows.size


def _attn_block(qp, kp, bias, first, q_ref, k_ref, v_ref, acc_ref, m_ref, l_ref):
    nk = bias.shape[1]
    ones = jnp.ones((nk, HEAD_DIM), BF16)
    heads_per_dot = 2 * BLK // nk
    zeros = jnp.zeros((nk, HEAD_DIM), BF16)
    for h0 in range(0, HEADS_PER_GROUP, heads_per_dot):
        hs = range(h0, h0 + heads_per_dot)
        qb = jnp.concatenate([_load_rows(q_ref, h, qp) for h in hs], axis=-1).astype(BF16)
        ks = [_load_rows(k_ref, h, kp).astype(BF16) for h in hs]
        vs = [_load_rows(v_ref, h, kp).astype(BF16) for h in hs]
        if heads_per_dot == 1:
            kb = ks[0]
            vb = jnp.concatenate([vs[0], ones], axis=-1)
            bias_w = bias
        else:
            kb = jnp.concatenate([jnp.concatenate([ks[0], zeros], axis=-1),
                                  jnp.concatenate([zeros, ks[1]], axis=-1)], axis=0)
            vb = jnp.concatenate([jnp.concatenate([vs[0], ones, zeros, zeros], axis=-1),
                                  jnp.concatenate([zeros, zeros, vs[1], ones], axis=-1)], axis=0)
            bias_w = jnp.concatenate([bias, bias], axis=-1)
        s = lax.dot_general(qb, kb, (((1,), (1,)), ((), ())), preferred_element_type=F32) + bias_w
        w = s.shape[1] // heads_per_dot
        m_olds, m_news = [], []
        for i, h in enumerate(hs):
            mb = jnp.max(s[:, i * w:(i + 1) * w], axis=-1, keepdims=True)
            if first:
                m_olds.append(None)
                m_news.append(jnp.broadcast_to(mb, (BLK, HEAD_DIM)))
            else:
                m_olds.append(_load_rows(m_ref, h, qp))
                m_news.append(jnp.maximum(m_olds[-1], mb))
        m_wide = jnp.concatenate([m for m in m_news for _ in range(w // HEAD_DIM)], axis=-1)
        p = jnp.exp2(s - m_wide).astype(BF16)
        pv = jnp.dot(p, vb, preferred_element_type=F32)
        for i, h in enumerate(hs):
            acc_new = pv[:, 2 * i * HEAD_DIM:(2 * i + 1) * HEAD_DIM]
            l_new = pv[:, (2 * i + 1) * HEAD_DIM:(2 * i + 2) * HEAD_DIM]
            if not first:
                alpha = jnp.exp2(m_olds[i] - m_news[i])
                acc_new = alpha * _load_rows(acc_ref, h, qp) + acc_new
                l_new = alpha * _load_rows(l_ref, h, qp) + l_new
            _store_rows(m_ref, h, qp, m_news[i])
            _store_rows(l_ref, h, qp, l_new)
            _store_rows(acc_ref, h, qp, acc_new)


def _attn_group(d, first, q_ref, k_ref, v_ref, bias_ref, acc_ref, m_ref, l_ref, stage_refs):
    if d == 16:
        def widen(c, _):
            rows = pl.ds(pl.multiple_of(c * ROW_CHUNK, ROW_CHUNK), ROW_CHUNK)
            for src_ref, dst_ref in zip((q_ref, k_ref, v_ref), stage_refs):
                for h in range(HEADS_PER_GROUP):
                    for b in range(CLS):
                        dst_ref[h, b, rows, :] = src_ref[h, b, rows, :].astype(F32)
            return _
        lax.fori_loop(0, CLS_LEN // ROW_CHUNK, widen, None)
        q_ref, k_ref, v_ref = stage_refs
    block = functools.partial(_attn_block, first=first, q_ref=q_ref, k_ref=k_ref, v_ref=v_ref,
                              acc_ref=acc_ref, m_ref=m_ref, l_ref=l_ref)
    if d == 16:
        for r in range(d):
            a, b = divmod(r, CLS)
            rows = [(b, pl.ds(a, BLK, stride=CLS))]
            block(rows, rows, bias_ref[0, :, BLK:])
    elif d == 4:
        for b in range(CLS):
            for bi in range(CLS_LEN // BLK):
                q0 = bi * BLK
                if bi == 0:
                    block([(b, pl.ds(q0, BLK))], [(b, pl.ds(q0, BLK))], bias_ref[0, :, BLK:])
                else:
                    block([(b, pl.ds(q0, BLK))], [(b, pl.ds(q0 - BLK, 2 * BLK))], bias_ref[0])
    else:
        piece = BLK // CLS
        for bi in range(SEQ // BLK):
            qp = [(b, pl.ds(bi * piece, piece)) for b in range(CLS)]
            if bi == 0:
                block(qp, qp, bias_ref[2, :, :BLK])
            else:
                kp = [(b, pl.ds((bi - 1) * piece, 2 * piece)) for b in range(CLS)]
                block(qp, kp, bias_ref[1])


def _attn_kernel(q_ref, k_ref, v_ref, z_ref, o_ref, bias_ref, acc_ref, m_ref, l_ref, nat_ref,
                 sq_ref, sk_ref, sv_ref):
    step = pl.program_id(1)
    _band_biases(bias_ref)

    for s in range(N_GROUPS):
        @pl.when(step == s)
        def _(s=s):
            _attn_group(ATTN_PATTERNS[N_GROUPS - 1 - s][1], s == 0, q_ref, k_ref, v_ref,
                        bias_ref, acc_ref, m_ref, l_ref, (sq_ref, sk_ref, sv_ref))

    @pl.when(step == N_GROUPS - 1)
    def _():
        half = CLS_LEN // 2
        for h in range(HEADS_PER_GROUP):
            for b in range(CLS):
                for k in range(2):
                    src = pl.ds(k * half, half)
                    nat_ref[h, pl.ds(b + k * half * CLS, half, stride=CLS), :] = (
                        acc_ref[h, b, src, :] / l_ref[h, b, src, :]
                        * _silu(z_ref[h, b, src, :].astype(F32)))

        def fin(c, _):
            rows = pl.ds(pl.multiple_of(c * ROW_CHUNK, ROW_CHUNK), ROW_CHUNK)
            for h in range(HEADS_PER_GROUP):
                o_ref[rows, h * HEAD_DIM:(h + 1) * HEAD_DIM] = nat_ref[h, rows, :].astype(o_ref.dtype)
            return _
        lax.fori_loop(0, SEQ // ROW_CHUNK, fin, None)


def _attn(qkv):
    nb = qkv.shape[1]
    n = nb * SEQ

    def head_spec(kind):
        return pl.BlockSpec((HEADS_PER_GROUP, None, CLS, CLS_LEN, HEAD_DIM),
                            lambda i, s: (kind * N_GROUPS + N_GROUPS - 1 - s, i, 0, 0, 0))

    slab = pltpu.VMEM((HEADS_PER_GROUP, CLS, CLS_LEN, HEAD_DIM), F32)
    return pl.pallas_call(
        _attn_kernel,
        grid=(nb, N_GROUPS),
        in_specs=[
            head_spec(0), head_spec(1), head_spec(2),
            pl.BlockSpec((HEADS_PER_GROUP, None, CLS, CLS_LEN, HEAD_DIM),
                         lambda i, s: (N_QKV_HEADS // HEADS_PER_GROUP, i, 0, 0, 0)),
        ],
        out_specs=pl.BlockSpec((SEQ, ATTN_OUT), lambda i, s: (i, 0)),
        out_shape=jax.ShapeDtypeStruct((n, ATTN_OUT), BF16),
        scratch_shapes=[pltpu.VMEM((3, BLK, 2 * BLK), F32), slab, slab, slab,
                        pltpu.VMEM((HEADS_PER_GROUP, SEQ, HEAD_DIM), F32), slab, slab, slab],
        compiler_params=pltpu.CompilerParams(
            dimension_semantics=("arbitrary", "arbitrary"),
            vmem_limit_bytes=VMEM_LIMIT_BYTES),
        name="attn",
    )(qkv, qkv, qkv, qkv)


def _merge_kernel(a_ref, cb_ref, cc_ref, cv_ref, zc_ref, hc_ref, hv_ref, mo_ref,
                  g0a_ref, g0b_ref, g1a_ref, g1b_ref, g2a_ref, g2b_ref,
                  x_ref, cw_ref,
                  wa_ref, wc_ref, wm_ref, wo_ref, o_ref, u_ref, mgp_ref, *, n_tiles):
    ts = MERGE_TS
    tiles_per_seq = SEQ // ts
    t = pl.program_id(0)
    seq_start = (jnp.minimum(t, n_tiles - 1) % tiles_per_seq) == 0

    @pl.when(t == 0)
    def _():
        mgp_ref[...] = jnp.zeros_like(mgp_ref)

    slot = t % 2
    o_ref[...] = x_ref[...] + jnp.dot(mgp_ref[1 - slot], wo_ref[...], preferred_element_type=F32)

    u = cc_ref[...].astype(F32) * cv_ref[...].astype(F32)
    uh = hc_ref[...].astype(F32) * hv_ref[...].astype(F32)
    u_ref[0:HALO, :] = jnp.where(seq_start, 0.0, uh)
    u_ref[HALO:HALO + ts, :] = u
    y = (cw_ref[0:1, :] * u
         + cw_ref[1:2, :] * u_ref[HALO - 1:HALO - 1 + ts, :]
         + cw_ref[2:3, :] * u_ref[HALO - 2:HALO - 2 + ts, :])
    c = (cb_ref[...].astype(F32) * y * zc_ref[...].astype(F32)).astype(BF16)

    ya = jnp.dot(a_ref[...], wa_ref[...], preferred_element_type=F32)
    yc = jnp.dot(c, wc_ref[...], preferred_element_type=F32)
    ym = jnp.dot(mo_ref[...], wm_ref[...], preferred_element_type=F32)

    half = D_MODEL // 2
    for part, (ga, gc, gm) in enumerate(((g0a_ref, g1a_ref, g2a_ref), (g0b_ref, g1b_ref, g2b_ref))):
        cols = slice(part * half, (part + 1) * half)
        mgp_ref[slot, :, cols] = (jax.nn.sigmoid(ga[...].astype(F32)) * ya[:, cols]
                           + jax.nn.sigmoid(gc[...].astype(F32)) * yc[:, cols]
                           + jax.nn.sigmoid(gm[...].astype(F32)) * ym[:, cols]).astype(BF16)


def _merge(a, rest, mo, x2, conv_w, wa, wc, wm, wo):
    n = x2.shape[0]
    ts = MERGE_TS
    n_tiles = n // ts
    half = D_MODEL // 2

    def cur(t):
        return jnp.minimum(t, n_tiles - 1)

    def prev(t):
        return jnp.maximum(t - 1, 0)

    def col_spec(col, width):
        return pl.BlockSpec((ts, width), lambda t: (cur(t), (col - REST_COL0) // width))

    def halo_spec(col):
        return pl.BlockSpec((HALO, CONV_WIDTH),
                            lambda t: (jnp.maximum(cur(t) * (ts // HALO) - 1, 0),
                                       (col - REST_COL0) // CONV_WIDTH))

    def const_spec(shape):
        return pl.BlockSpec(shape, lambda t: (0,) * len(shape), pipeline_mode=pl.Buffered(1))

    in_specs = [
        pl.BlockSpec((ts, ATTN_OUT), lambda t: (cur(t), 0)),
        col_spec(COL_CB, CONV_WIDTH), col_spec(COL_CC, CONV_WIDTH),
        col_spec(COL_CV, CONV_WIDTH), col_spec(COL_ZC, CONV_WIDTH),
        halo_spec(COL_CC), halo_spec(COL_CV),
        pl.BlockSpec((ts, MEM_W), lambda t: (cur(t), 0)),
    ] + [col_spec(COL_G + k * half, half) for k in range(2 * N_BRANCH)] + [
        pl.BlockSpec((ts, D_MODEL), lambda t: (prev(t), 0)),
        const_spec((CONV_K, CONV_WIDTH)),
        const_spec((ATTN_OUT, D_MODEL)), const_spec((CONV_WIDTH, D_MODEL)),
        const_spec((MEM_W, D_MODEL)), const_spec((D_MODEL, D_MODEL)),
    ]
    n_head, n_gate = 6, 2 * N_BRANCH
    return pl.pallas_call(
        functools.partial(_merge_kernel, n_tiles=n_tiles),
        grid=(n_tiles + 1,),
        in_specs=in_specs,
        out_specs=pl.BlockSpec((ts, D_MODEL), lambda t: (prev(t), 0)),
        out_shape=jax.ShapeDtypeStruct((n, D_MODEL), F32),
        scratch_shapes=[pltpu.VMEM((HALO + ts, CONV_WIDTH), F32),
                        pltpu.VMEM((2, ts, D_MODEL), BF16)],
        compiler_params=pltpu.CompilerParams(
            dimension_semantics=("arbitrary",),
            vmem_limit_bytes=VMEM_LIMIT_BYTES),
        name="merge",
    )(a, *([rest] * n_head), mo, *([rest] * n_gate), x2, conv_w, wa, wc, wm, wo)


def kernel(x, mem, norm_g, mem_norm_g, w_in, attn_q_norm, attn_k_norm, conv_w, mem_w_kv,
           mem_q_norm, mem_k_norm, w_br_attn, w_br_conv, w_br_mem, w_out):
    b, s, d = x.shape
    assert (s, d) == (SEQ, D_MODEL) and w_in.shape == (D_MODEL, IN_COLS)
    x2 = x.reshape(b * s, d)
    scale = HEAD_DIM ** -0.5 * LOG2E
    head_gains = jnp.concatenate([jnp.repeat(attn_q_norm * scale, HEADS_PER_GROUP, axis=0),
                                  jnp.repeat(attn_k_norm, HEADS_PER_GROUP, axis=0),
                                  jnp.ones(((N_HEAD_TILES - N_NORM_TILES) * HEADS_PER_TILE, HEAD_DIM), F32)],
                                 axis=0)
    mem_q_gain = mem_q_norm.reshape(1, MEM_HEAD_DIM) * (MEM_HEAD_DIM ** -0.5 * LOG2E)
    qkv, rest = _in_proj(x2, norm_g.reshape(1, d), head_gains, mem_q_gain, w_in.astype(BF16))
    mo = _mem_branch(mem, mem_norm_g.reshape(1, d), mem_w_kv.astype(BF16),
                     mem_k_norm.reshape(1, MEM_HEAD_DIM), rest)
    a = _attn(qkv)
    out = _merge(a, rest, mo, x2, conv_w,
                 w_br_attn.astype(BF16), w_br_conv.astype(BF16), w_br_mem.astype(BF16),
                 w_out.astype(BF16))
    return out.reshape(b, s, d)
```
